```python
import math
import jax, jax.numpy as jnp
from jax import lax
import numpy as np


D_MODEL = 1024
BATCH = 32
SEQ = 2048
DEPTH = 2

GRID_W = 64
CTX_LEN = 256
N_EVEN = (DEPTH + 1) // 2
N_ODD = DEPTH // 2
EPS = 1e-6

HY_W = 512
HY_EMB = 33
HY_BANDS = (HY_EMB - 1) // 2
HY_ORDER = 64
HY_DECAY_TARGET = 1e-2
HY_FAST_PCT = 0.3
HY_SLOW_PCT = 1.5
HY_FILTER_GAIN = 0.05
RET_W = 512
RET_H = 4
RET_HD = RET_W // RET_H
CHUNK = 128
EVEN_IN = 4 * HY_W + 4 * RET_W
EVEN_OUT = HY_W + RET_W
N_HEADS = 8
N_KV = 2
HEAD_DIM = 128
GROUP = N_HEADS // N_KV
ATT_W = N_HEADS * HEAD_DIM
KV_W = N_KV * HEAD_DIM
ODD_IN = 2 * ATT_W + 2 * KV_W
Q_BLOCK = 128
ROPE_THETA = 10000.0

kernel_name = 'hybrid_hyena_retention_gqa_dit_prefix'

F32 = jnp.float32


def rmsnorm(x, g):
    xf = x.astype(F32)
    y = xf * lax.rsqrt(jnp.mean(xf * xf, axis=-1, keepdims=True) + EPS)
    return (y * g.astype(F32)).astype(x.dtype)


def ada_params(cond, w, b):
    m = jax.nn.silu(cond) @ w + b
    shift, scale, gate = jnp.split(m, 3, axis=-1)
    return shift[..., None, :], scale[..., None, :], gate[..., None, :]


def short_conv(u, w, b):
    up = jnp.pad(u, ((0, 0), (1, 1), (0, 0)))
    return up[:, :-2] * w[0] + up[:, 1:-1] * w[1] + up[:, 2:] * w[2] + b


def hyena_filters(L, w1, b1, freq, w2, b2, w3):
    n = jnp.arange(L, dtype=F32)
    t = n / max(L - 1, 1)
    bands = jnp.linspace(1e-4, HY_BANDS - 1, HY_BANDS, dtype=F32)
    ang = (2.0 * math.pi / L) * n[:, None] * bands[None, :]
    z = jnp.concatenate([t[:, None], jnp.cos(ang), -jnp.sin(ang)], axis=-1)
    f = freq.astype(F32)
    h = jnp.sin(f * (z @ w1.astype(F32) + b1.astype(F32)))
    h = jnp.sin(f * (h @ w2.astype(F32) + b2.astype(F32)))
    h = h @ w3.astype(F32)
    max_decay = math.log(HY_DECAY_TARGET) / HY_FAST_PCT
    min_decay = math.log(HY_DECAY_TARGET) / HY_SLOW_PCT
    deltas = jnp.abs(jnp.linspace(min_decay, max_decay, HY_W, dtype=F32))
    window = jnp.exp(-t[:, None] * deltas[None, :])
    h = h.reshape(L, 2, HY_W) * window[:, None, :]
    return h[:, 0], h[:, 1]


def long_conv_bidir(u, h_fwd, h_bwd, bias):
    L = u.shape[1]
    uf = u.astype(F32)
    k = jnp.concatenate([h_fwd, jnp.zeros_like(h_fwd[:1]), h_bwd[:0:-1]], axis=0)
    spec = jnp.fft.rfft(uf, n=2 * L, axis=1) * jnp.fft.rfft(k, n=2 * L, axis=0)[None]
    y = jnp.fft.irfft(spec, n=2 * L, axis=1)[:, :L]
    return (y + uf * bias.astype(F32)).astype(u.dtype)


def hyena_mix(z, conv_w, conv_b, filt, bias):
    z = short_conv(z, conv_w, conv_b)
    x0, x1, v = jnp.split(z, 3, axis=-1)
    h_f, h_b = hyena_filters(z.shape[1], *filt)
    return x0 * long_conv_bidir(x1 * v, h_f, h_b, bias)


def retention_chunks(q, k, v, log_g, state):
    B, H, L, d = q.shape
    nc = L // CHUNK
    idx = jnp.arange(CHUNK, dtype=F32)
    diff = idx[:, None] - idx[None, :]
    decay_in = jnp.where(diff >= 0, jnp.exp(log_g[:, None, None] * jnp.maximum(diff, 0.0)), 0.0).astype(q.dtype)
    q_decay = jnp.exp(log_g[:, None] * (idx + 1.0))[None, :, :, None].astype(q.dtype)
    k_decay = jnp.exp(log_g[:, None] * (CHUNK - 1.0 - idx))[None, :, :, None].astype(q.dtype)
    chunk_decay = jnp.exp(log_g * CHUNK)[None, :, None, None].astype(q.dtype)

    def to_chunks(a):
        return a.reshape(B, H, nc, CHUNK, d).transpose(2, 0, 1, 3, 4)

    def step(s, blk):
        qb, kb, vb = blk
        scores = jnp.einsum('bhid,bhjd->bhij', qb, kb) * decay_in
        o = jnp.einsum('bhij,bhjd->bhid', scores, vb) + jnp.einsum('bhid,bhde->bhie', qb, s) * q_decay
        s = s * chunk_decay + jnp.einsum('bhjd,bhje->bhde', kb * k_decay, vb)
        return s, o

    s, o = lax.scan(step, state, (to_chunks(q), to_chunks(k), to_chunks(v)))
    return o.transpose(1, 2, 0, 3, 4).reshape(B, H, L, d), s


def retention_bidir(q, k, v, log_f, log_b, s_f, s_b):
    o_f, s_f = retention_chunks(q, k, v, log_f, s_f)
    fl = lambda a: jnp.flip(a, axis=2)
    o_b, s_b = retention_chunks(fl(q), fl(k), fl(v), log_b, s_b)
    return o_f + fl(o_b), s_f, s_b


def head_norm(o, g):
    of = o.astype(F32)
    mu = jnp.mean(of, axis=-1, keepdims=True)
    var = jnp.mean((of - mu) ** 2, axis=-1, keepdims=True)
    y = (of - mu) * lax.rsqrt(var + EPS)
    B, H, L, d = o.shape
    y = y.transpose(0, 2, 1, 3).reshape(B, L, H * d) * g.astype(F32)
    return y.astype(o.dtype)


def even_mixer(hx, hc, in_w, out_w, conv_w, conv_b, filt, hy_bias, a_f, a_b, ret_g, ctx_out):
    cuts = [3 * HY_W, 4 * HY_W, 4 * HY_W + RET_W, 4 * HY_W + 2 * RET_W, 4 * HY_W + 3 * RET_W]
    px = jnp.split(hx @ in_w, cuts, axis=-1)
    pc = jnp.split(hc @ in_w, cuts, axis=-1)
    B = hx.shape[0]
    heads = lambda a: a.reshape(B, a.shape[1], RET_H, RET_HD).transpose(0, 2, 1, 3)
    k_scale = RET_HD ** -0.5
    log_f = -jnp.exp(a_f.astype(F32))
    log_b = -jnp.exp(a_b.astype(F32))
    zero = jnp.zeros((B, RET_H, RET_HD, RET_HD), hx.dtype)
    o_c, s_f, s_b = retention_bidir(heads(pc[2]), heads(pc[3]) * k_scale, heads(pc[4]), log_f, log_b, zero, zero)
    o_x, _, _ = retention_bidir(heads(px[2]), heads(px[3]) * k_scale, heads(px[4]), log_f, log_b, s_f, s_b)

    def merge(p, o_ret):
        y_hy = hyena_mix(p[0], conv_w, conv_b, filt, hy_bias) * jax.nn.silu(p[1])
        y_ret = head_norm(o_ret, ret_g) * jax.nn.silu(p[5])
        return jnp.concatenate([y_hy, y_ret], axis=-1) @ out_w

    yx = merge(px, o_x)
    yc = merge(pc, o_c) if ctx_out else None
    return yx, yc


def grid_angles(rows):
    r = jnp.repeat(jnp.arange(rows, dtype=F32), GRID_W)
    col = jnp.tile(jnp.arange(GRID_W, dtype=F32), rows)
    half = HEAD_DIM // 2
    inv = ROPE_THETA ** (-jnp.arange(0, half, 2, dtype=F32) / half)
    return jnp.concatenate([r[:, None] * inv, col[:, None] * inv], axis=-1)


def rope_2d(x, ang):
    cos = jnp.cos(ang)[:, None, :].astype(x.dtype)
    sin = jnp.sin(ang)[:, None, :].astype(x.dtype)
    xr = x.reshape(*x.shape[:-1], HEAD_DIM // 2, 2)
    x0, x1 = xr[..., 0], xr[..., 1]
    return jnp.stack([x0 * cos - x1 * sin, x0 * sin + x1 * cos], axis=-1).reshape(x.shape)


def gqa(q, k, v):
    B, Lq = q.shape[:2]
    qg = q.reshape(B, Lq, N_KV, GROUP, HEAD_DIM)
    s = jnp.einsum('bqkgd,bskd->bkgqs', qg, k).astype(F32) * (HEAD_DIM ** -0.5)
    p = jax.nn.softmax(s, axis=-1).astype(v.dtype)
    o = jnp.einsum('bkgqs,bskd->bqkgd', p, v)
    return o.reshape(B, Lq, ATT_W)


def blocked_gqa(q, k, v):
    B, L = q.shape[:2]
    nb = L // Q_BLOCK
    qb = q.reshape(B, nb, Q_BLOCK, N_HEADS, HEAD_DIM).swapaxes(0, 1)
    o = lax.map(lambda qi: gqa(qi, k, v), qb)
    return o.swapaxes(0, 1).reshape(B, L, ATT_W)


def attn_mixer(hx, hc, ang, in_w, out_w, q_norm, k_norm, ctx_out):
    B, L, _ = hx.shape
    Lc = hc.shape[1]
    cuts = [ATT_W, ATT_W + KV_W, ATT_W + 2 * KV_W]
    q, k, v, g = jnp.split(hx @ in_w, cuts, axis=-1)
    q = rope_2d(rmsnorm(q.reshape(B, L, N_HEADS, HEAD_DIM), q_norm), ang)
    k = rope_2d(rmsnorm(k.reshape(B, L, N_KV, HEAD_DIM), k_norm), ang)
    v = v.reshape(B, L, N_KV, HEAD_DIM)
    if ctx_out:
        qc, kc, vc, gc = jnp.split(hc @ in_w, cuts, axis=-1)
    else:
        kc, vc = jnp.split(hc @ in_w[:, ATT_W:ATT_W + 2 * KV_W], 2, axis=-1)
    kc = rmsnorm(kc.reshape(B, Lc, N_KV, HEAD_DIM), k_norm)
    vc = vc.reshape(B, Lc, N_KV, HEAD_DIM)
    o = blocked_gqa(q, jnp.concatenate([kc, k], axis=1), jnp.concatenate([vc, v], axis=1))
    yx = (jax.nn.silu(g) * o) @ out_w
    if ctx_out:
        qc = rmsnorm(qc.reshape(B, Lc, N_HEADS, HEAD_DIM), q_norm)
        yc = (jax.nn.silu(gc) * gqa(qc, kc, vc)) @ out_w
    else:
        yc = None
    return yx, yc


def setup_inputs(seed: int = 0) -> dict:
    key = jax.random.key(seed)
    ks = iter(jax.random.split(key, 32))
    nrm = lambda shape, s: jax.random.normal(next(ks), shape, F32) * s
    D = D_MODEL
    a0 = jnp.log(-jnp.log1p(-(2.0 ** (-5.0 - jnp.arange(RET_H, dtype=F32)))))
    return {
        'x': nrm((BATCH, SEQ, D), 1.0),
        'c': nrm((BATCH, D), 1.0),
        'ctx': nrm((BATCH, CTX_LEN, D), 1.0),
        'c_ctx': nrm((D,), 1.0),
        'norm_g': 1.0 + nrm((DEPTH, D), 0.02),
        'ada_w': nrm((DEPTH, D, 3 * D), 0.5 * D ** -0.5),
        'ada_b': nrm((DEPTH, 3 * D), 0.02),
        'er_in_w': nrm((N_EVEN, D, EVEN_IN), D ** -0.5),
        'er_out_w': nrm((N_EVEN, EVEN_OUT, D), EVEN_OUT ** -0.5),
        'hy_conv_w': nrm((N_EVEN, 3, 3 * HY_W), 3 ** -0.5),
        'hy_conv_b': nrm((N_EVEN, 3 * HY_W), 0.02),
        'hy_f_w1': nrm((N_EVEN, HY_EMB, HY_ORDER), HY_EMB ** -0.5),
        'hy_f_b1': nrm((N_EVEN, HY_ORDER), 0.02),
        'hy_f_freq': 1.0 + nrm((N_EVEN, HY_ORDER), 0.02),
        'hy_f_w2': nrm((N_EVEN, HY_ORDER, HY_ORDER), HY_ORDER ** -0.5),
        'hy_f_b2': nrm((N_EVEN, HY_ORDER), 0.02),
        'hy_f_w3': nrm((N_EVEN, HY_ORDER, 2 * HY_W), HY_FILTER_GAIN * HY_ORDER ** -0.5),
        'hy_bias': nrm((N_EVEN, HY_W), 0.5),
        'ret_decay_f': a0 + nrm((N_EVEN, RET_H), 0.1),
        'ret_decay_b': a0 + nrm((N_EVEN, RET_H), 0.1),
        'ret_norm_g': 1.0 + nrm((N_EVEN, RET_W), 0.02),
        'at_in_w': nrm((N_ODD, D, ODD_IN), D ** -0.5),
        'at_out_w': nrm((N_ODD, ATT_W, D), ATT_W ** -0.5),
        'at_q_norm': 1.0 + nrm((N_ODD, HEAD_DIM), 0.02),
        'at_k_norm': 1.0 + nrm((N_ODD, HEAD_DIM), 0.02),
        'final_norm_g': 1.0 + nrm((D,), 0.02),
    }


def reference(x, c, ctx, c_ctx, norm_g, ada_w, ada_b, er_in_w, er_out_w, hy_conv_w, hy_conv_b,
              hy_f_w1, hy_f_b1, hy_f_freq, hy_f_w2, hy_f_b2, hy_f_w3, hy_bias,
              ret_decay_f, ret_decay_b, ret_norm_g, at_in_w, at_out_w, at_q_norm, at_k_norm, final_norm_g):
    rows = x.shape[1] // GRID_W
    ang = grid_angles(rows)
    for i in range(DEPTH):
        ctx_out = i < DEPTH - 1
        sx, scx, gx = ada_params(c, ada_w[i], ada_b[i])
        sc, scc, gc = ada_params(c_ctx, ada_w[i], ada_b[i])
        hx = rmsnorm(x, norm_g[i]) * (1 + scx) + sx
        hc = rmsnorm(ctx, norm_g[i]) * (1 + scc) + sc
        j = i // 2
        if i % 2 == 0:
            filt = (hy_f_w1[j], hy_f_b1[j], hy_f_freq[j], hy_f_w2[j], hy_f_b2[j], hy_f_w3[j])
            yx, yc = even_mixer(hx, hc, er_in_w[j], er_out_w[j], hy_conv_w[j], hy_conv_b[j], filt, hy_bias[j],
                                ret_decay_f[j], ret_decay_b[j], ret_norm_g[j], ctx_out)
        else:
            yx, yc = attn_mixer(hx, hc, ang, at_in_w[j], at_out_w[j], at_q_norm[j], at_k_norm[j], ctx_out)
        x = x + gx * yx
        if ctx_out:
            ctx = ctx + gc * yc
    return rmsnorm(x, final_norm_g)
```

```python
import functools
import math

import jax
import jax.numpy as jnp
from jax import lax
from jax.experimental import pallas as pl
from jax.experimental.pallas import tpu as pltpu

F32 = jnp.float32
BF16 = jnp.bfloat16
HIGHEST = lax.Precision.HIGHEST

EPS = 1e-6
GRID_W = 64
HY_W = 512
HY_EMB = 33
HY_BANDS = (HY_EMB - 1) // 2
HY_ORDER = 64
HY_DECAY_TARGET = 1e-2
HY_FAST_PCT = 0.3
HY_SLOW_PCT = 1.5
RET_W = 512
RET_H = 4
RET_HD = RET_W // RET_H
CHUNK = 128
N_HEADS = 8
N_KV = 2
HEAD_DIM = 128
GROUP = N_HEADS // N_KV
ATT_W = N_HEADS * HEAD_DIM
KV_W = N_KV * HEAD_DIM
ROPE_THETA = 10000.0

LANES = 128
TOEP = 256
VMEM_LIMIT = 56 * 1024 * 1024


def _params(sem):
    return pltpu.CompilerParams(dimension_semantics=sem, vmem_limit_bytes=VMEM_LIMIT)


def _silu(x):
    return x * (1.0 / (1.0 + jnp.exp(-x)))


def _dot(a, b):
    return jnp.dot(a, b, preferred_element_type=F32)


def _dot_nt(a, b):
    return lax.dot_general(a, b, (((1,), (1,)), ((), ())), preferred_element_type=F32)


def _dot_tn(a, b):
    return lax.dot_general(a, b, (((0,), (0,)), ((), ())), preferred_element_type=F32)


def _mod_norm(x, g, shift, scale):
    ms = jnp.mean(x * x, axis=-1, keepdims=True)
    return (x * lax.rsqrt(ms + EPS) * g) * (1.0 + scale) + shift


def _ada_kernel(cond_ref, w_ref, b_ref, o_ref):
    s = _silu(cond_ref[...])
    o_ref[...] = jnp.dot(s, w_ref[...], preferred_element_type=F32, precision=HIGHEST) + b_ref[...]


def _ada(cond, ada_w, ada_b):
    depth, d, n = ada_w.shape
    rows = cond.shape[0]
    tn = 1024
    return pl.pallas_call(
        _ada_kernel,
        grid=(depth, n // tn),
        in_specs=[
            pl.BlockSpec((rows, d), lambda i, j: (0, 0)),
            pl.BlockSpec((None, d, tn), lambda i, j: (i, 0, j)),
            pl.BlockSpec((None, 1, tn), lambda i, j: (i, 0, j)),
        ],
        out_specs=pl.BlockSpec((None, rows, tn), lambda i, j: (i, 0, j)),
        out_shape=jax.ShapeDtypeStruct((depth, rows, n), F32),
        compiler_params=_params(("arbitrary", "arbitrary")),
        name="ada",
    )(cond, ada_w, ada_b.reshape(depth, 1, n))


def _inproj0_kernel(x_ref, mod_ref, g_ref, whyt_ref, wret_ref, zhyt_ref, zret_ref):
    d = x_ref.shape[-1]
    mod = mod_ref[...]
    h = _mod_norm(x_ref[...], g_ref[...], mod[:, 0:d], mod[:, d:2 * d]).astype(BF16)
    nh = whyt_ref.shape[0]
    step = 512
    for n in range(0, nh, step):
        zhyt_ref[n:n + step, :] = _dot_nt(whyt_ref[n:n + step, :], h).astype(BF16)
    nr = wret_ref.shape[1]
    for n in range(0, nr, step):
        zret_ref[:, n:n + step] = _dot(h, wret_ref[:, n:n + step]).astype(BF16)


def _inproj0(x2, mod4, layer, mod_row, g, whyt, wret, batch, seq, tm):
    d = x2.shape[-1]
    nt = seq // tm
    nh, nr = whyt.shape[0], wret.shape[1]
    return pl.pallas_call(
        _inproj0_kernel,
        grid=(batch, nt),
        in_specs=[
            pl.BlockSpec((tm, d), lambda b, t: (b * nt + t, 0)),
            pl.BlockSpec((None, None, 1, 3 * d), lambda b, t: (layer, mod_row(b), 0, 0)),
            pl.BlockSpec((1, d), lambda b, t: (0, 0)),
            pl.BlockSpec((nh, d), lambda b, t: (0, 0)),
            pl.BlockSpec((d, nr), lambda b, t: (0, 0)),
        ],
        out_specs=[
            pl.BlockSpec((nh, tm), lambda b, t: (0, b * nt + t)),
            pl.BlockSpec((tm, nr), lambda b, t: (b * nt + t, 0)),
        ],
        out_shape=[
            jax.ShapeDtypeStruct((nh, batch * seq), BF16),
            jax.ShapeDtypeStruct((batch * seq, nr), BF16),
        ],
        compiler_params=_params(("arbitrary", "arbitrary")),
        name="inproj0",
    )(x2, mod4, g, whyt, wret)


def _filter_kernel(w1t_ref, w1c_ref, w1s_ref, b1_ref, f_ref, w2_ref, b2_ref, w3_ref, bias_ref, o_ref, *, seq):
    two_l = o_ref.shape[-1]
    cb = o_ref.shape[0]
    lag = lax.broadcasted_iota(jnp.int32, (1, two_l), 1) - seq
    n = jnp.abs(lag).astype(F32)
    t = n / float(max(seq - 1, 1))
    band_i = lax.broadcasted_iota(jnp.int32, (HY_BANDS, 1), 0).astype(F32)
    bands = 1e-4 + band_i * ((HY_BANDS - 1 - 1e-4) / (HY_BANDS - 1))
    ang = (2.0 * math.pi / seq) * n * bands
    f = f_ref[...]
    pre = (w1t_ref[...] * t
           + jnp.dot(w1c_ref[...], jnp.cos(ang), preferred_element_type=F32, precision=HIGHEST)
           + jnp.dot(w1s_ref[...], -jnp.sin(ang), preferred_element_type=F32, precision=HIGHEST))
    h = jnp.sin(f * (pre + b1_ref[...]))
    h = jnp.sin(f * (jnp.dot(w2_ref[...], h, preferred_element_type=F32, precision=HIGHEST) + b2_ref[...]))
    h_f = jnp.dot(w3_ref[0], h, preferred_element_type=F32, precision=HIGHEST)
    h_b = jnp.dot(w3_ref[1], h, preferred_element_type=F32, precision=HIGHEST)
    max_decay = math.log(HY_DECAY_TARGET) / HY_FAST_PCT
    min_decay = math.log(HY_DECAY_TARGET) / HY_SLOW_PCT
    ch = (lax.broadcasted_iota(jnp.int32, (cb, 1), 0) + pl.program_id(0) * cb).astype(F32)
    deltas = jnp.abs(min_decay + ch * ((max_decay - min_decay) / (HY_W - 1)))
    window = jnp.exp(-t * deltas)
    k = jnp.where(lag >= 0, h_f, h_b) * window
    k = jnp.where(lag == -seq, 0.0, k)
    o_ref[...] = k + jnp.where(lag == 0, bias_ref[...], 0.0)


def _filters(seq, w1, b1, freq, w2, b2, w3, bias):
    cb = 128
    w1t = w1.T
    args = (
        w1t[:, 0:1], w1t[:, 1:1 + HY_BANDS], w1t[:, 1 + HY_BANDS:],
        b1.reshape(HY_ORDER, 1), freq.reshape(HY_ORDER, 1), w2.T, b2.reshape(HY_ORDER, 1),
        w3.T.reshape(2, HY_W, HY_ORDER), bias.reshape(HY_W, 1),
    )
    full = lambda a: pl.BlockSpec(a.shape, lambda i: (0,) * a.ndim)
    in_specs = [full(a) for a in args[:7]] + [
        pl.BlockSpec((2, cb, HY_ORDER), lambda i: (0, i, 0)),
        pl.BlockSpec((cb, 1), lambda i: (i, 0)),
    ]
    return pl.pallas_call(
        functools.partial(_filter_kernel, seq=seq),
        grid=(HY_W // cb,),
        in_specs=in_specs,
        out_specs=pl.BlockSpec((cb, 2 * seq), lambda i: (i, 0)),
        out_shape=jax.ShapeDtypeStruct((HY_W, 2 * seq), F32),
        compiler_params=_params(("arbitrary",)),
        name="hyena_filters",
    )(*args)


def _hyena_kernel(cw_ref, cb_ref, x0_ref, x1_ref, v_ref, gate_ref, kext_ref, out_ref,
                  r_ref, t_ref, u2_ref, y2_ref, *, seq, batch):
    cblk = out_ref.shape[0]
    nb = seq // TOEP
    nq = 2 * seq // LANES
    half = seq // LANES
    lane = lax.broadcasted_iota(jnp.int32, (1, seq), 1)
    first = lane == 0
    last = lane == seq - 1
    row_i = lax.broadcasted_iota(jnp.int32, (LANES, LANES), 0)
    col_i = lax.broadcasted_iota(jnp.int32, (LANES, LANES), 1)
    upper = col_i >= row_i
    c0 = pl.program_id(0) * cblk

    def short_conv(ref, c, part):
        a = ref[c].astype(F32)
        ch = part * HY_W + c0 + c
        left = jnp.where(first, 0.0, pltpu.roll(a, 1, 1))
        right = jnp.where(last, 0.0, pltpu.roll(a, seq - 1, 1))
        return left * cw_ref[0, ch] + a * cw_ref[1, ch] + right * cw_ref[2, ch] + cb_ref[ch]

    def channel(c, carry):
        krow = kext_ref[pl.ds(c, 1), :]
        for q in range(nq):
            a = jnp.broadcast_to(krow[:, q * LANES:(q + 1) * LANES], (LANES, LANES))
            r_ref[q] = pltpu.roll(a, 0, 1, stride=1, stride_axis=0)
        for e in range(-(2 * nb - 1), 2 * nb):
            q0 = half + e
            g = jnp.where(upper, r_ref[q0], r_ref[q0 - 1]).astype(BF16)
            if e % 2 == 0:
                d = e // 2
                t_ref[d + nb - 1, 0:LANES, 0:LANES] = g
                t_ref[d + nb - 1, LANES:TOEP, LANES:TOEP] = g
            else:
                d = (e - 1) // 2
                if abs(d) <= nb - 1:
                    t_ref[d + nb - 1, 0:LANES, LANES:TOEP] = g
                d = (e + 1) // 2
                if abs(d) <= nb - 1:
                    t_ref[d + nb - 1, LANES:TOEP, 0:LANES] = g
        u = (short_conv(x1_ref, c, 1) * short_conv(v_ref, c, 2)).astype(BF16)
        for j in range(nb):
            u2_ref[j * batch:(j + 1) * batch, :] = u[:, j * TOEP:(j + 1) * TOEP]
        y2_ref[...] = _dot(u2_ref[...], t_ref[nb - 1])
        for d in list(range(1, nb)) + list(range(-(nb - 1), 0)):
            j_lo, j_hi = max(0, -d), nb - max(0, d)
            res = _dot(u2_ref[j_lo * batch:j_hi * batch, :], t_ref[d + nb - 1])
            y2_ref[(j_lo + d) * batch:(j_hi + d) * batch, :] += res
        x0 = short_conv(x0_ref, c, 0)
        gate = _silu(gate_ref[c].astype(F32))
        for j in range(nb):
            sl = slice(j * TOEP, (j + 1) * TOEP)
            out_ref[c, :, sl] = (x0[:, sl] * y2_ref[j * batch:(j + 1) * batch, :] * gate[:, sl]).astype(BF16)
        return carry

    lax.fori_loop(0, cblk, channel, 0)


def _hyena(zhyt, conv_w, conv_b, kext, batch, seq):
    cblk = 8
    nb = seq // TOEP
    z4 = zhyt.reshape(4, HY_W, batch, seq)
    part = lambda p: pl.BlockSpec((None, cblk, batch, seq), lambda i: (p, i, 0, 0))
    smem = pl.BlockSpec(memory_space=pltpu.SMEM)
    return pl.pallas_call(
        functools.partial(_hyena_kernel, seq=seq, batch=batch),
        grid=(HY_W // cblk,),
        in_specs=[smem, smem, part(0), part(1), part(2), part(3),
                  pl.BlockSpec((cblk, 2 * seq), lambda i: (i, 0))],
        out_specs=pl.BlockSpec((cblk, batch, seq), lambda i: (i, 0, 0)),
        out_shape=jax.ShapeDtypeStruct((HY_W, batch, seq), BF16),
        scratch_shapes=[
            pltpu.VMEM((2 * seq // LANES, LANES, LANES), F32),
            pltpu.VMEM((2 * nb - 1, TOEP, TOEP), BF16),
            pltpu.VMEM((nb * batch, TOEP), BF16),
            pltpu.VMEM((nb * batch, TOEP), F32),
        ],
        compiler_params=_params(("arbitrary",)),
        name="hyena",
    )(conv_w, conv_b, z4, z4, z4, z4, kext)


def _retention_kernel(af_ref, ab_ref, qc_ref, kc_ref, vc_ref, gc_ref, qx_ref, kx_ref, vx_ref, gx_ref,
                      ng_ref, oc_ref, ox_ref, s_ref, ofc_ref, obc_ref, ofx_ref, obx_ref):
    h = pl.program_id(1)
    k_scale = RET_HD ** -0.5
    ri = lax.broadcasted_iota(jnp.int32, (CHUNK, CHUNK), 0).astype(F32)
    ci = lax.broadcasted_iota(jnp.int32, (CHUNK, CHUNK), 1).astype(F32)
    rcol = lax.broadcasted_iota(jnp.int32, (CHUNK, 1), 0).astype(F32)
    log_f = -jnp.exp(jnp.full((1, 1), af_ref[h], F32))
    log_b = -jnp.exp(jnp.full((1, 1), ab_ref[h], F32))
    diff = ri - ci
    dec_f = jnp.where(diff >= 0, jnp.exp(log_f * jnp.maximum(diff, 0.0)), 0.0) * k_scale
    dec_b = jnp.where(diff <= 0, jnp.exp(log_b * jnp.maximum(-diff, 0.0)), 0.0) * k_scale
    qd_f = jnp.exp(log_f * (rcol + 1.0))
    qd_b = jnp.exp(log_b * (CHUNK - rcol))
    kd_f = jnp.exp(log_f * (CHUNK - 1.0 - rcol)) * k_scale
    kd_b = jnp.exp(log_b * rcol) * k_scale
    cd_f = jnp.exp(log_f * CHUNK)
    cd_b = jnp.exp(log_b * CHUNK)

    s_ref[...] = jnp.zeros_like(s_ref)

    def one(q_ref, k_ref, v_ref, o_ref, row, si, dec, qd, kd, cd):
        q = q_ref[pl.ds(row, CHUNK), :]
        k = k_ref[pl.ds(row, CHUNK), :]
        v = v_ref[pl.ds(row, CHUNK), :]
        s = s_ref[si]
        scores = _dot_nt(q, k) * dec
        o_ref[pl.ds(row, CHUNK), :] = _dot(scores.astype(BF16), v) + _dot(q, s.astype(BF16)) * qd
        kdt = jnp.transpose(k.astype(F32) * kd).astype(BF16)
        s_ref[si] = s * cd + _dot(kdt, v)

    def sweep(q_ref, k_ref, v_ref, of_ref, ob_ref):
        nchunk = q_ref.shape[0] // CHUNK

        def body(t, carry):
            row_f = pl.multiple_of(t * CHUNK, CHUNK)
            row_b = pl.multiple_of((nchunk - 1 - t) * CHUNK, CHUNK)
            one(q_ref, k_ref, v_ref, of_ref, row_f, 0, dec_f, qd_f, kd_f, cd_f)
            one(q_ref, k_ref, v_ref, ob_ref, row_b, 1, dec_b, qd_b, kd_b, cd_b)
            return carry

        lax.fori_loop(0, nchunk, body, 0)

    sweep(qc_ref, kc_ref, vc_ref, ofc_ref, obc_ref)
    sweep(qx_ref, kx_ref, vx_ref, ofx_ref, obx_ref)

    def finish(of_ref, ob_ref, g_ref, out_ref):
        nchunk = of_ref.shape[0] // CHUNK

        def body(t, carry):
            row = pl.multiple_of(t * CHUNK, CHUNK)
            o = of_ref[pl.ds(row, CHUNK), :] + ob_ref[pl.ds(row, CHUNK), :]
            mu = jnp.mean(o, axis=-1, keepdims=True)
            oc = o - mu
            var = jnp.mean(oc * oc, axis=-1, keepdims=True)
            y = oc * lax.rsqrt(var + EPS) * ng_ref[...]
            gate = _silu(g_ref[pl.ds(row, CHUNK), :].astype(F32))
            out_ref[pl.ds(row, CHUNK), :] = (y * gate).astype(BF16)
            return carry

        lax.fori_loop(0, nchunk, body, 0)

    finish(ofc_ref, obc_ref, gc_ref, oc_ref)
    finish(ofx_ref, obx_ref, gx_ref, ox_ref)


def _retention(zret_c, zret_x, a_f, a_b, norm_g, batch, ctx_len, seq):
    hd = RET_HD
    col = lambda part, rows: pl.BlockSpec((rows, hd), lambda b, h: (b, part * RET_H + h))
    smem = pl.BlockSpec(memory_space=pltpu.SMEM)
    return pl.pallas_call(
        _retention_kernel,
        grid=(batch, RET_H),
        in_specs=[smem, smem,
                  col(0, ctx_len), col(1, ctx_len), col(2, ctx_len), col(3, ctx_len),
                  col(0, seq), col(1, seq), col(2, seq), col(3, seq),
                  pl.BlockSpec((1, hd), lambda b, h: (0, h))],
        out_specs=[pl.BlockSpec((ctx_len, hd), lambda b, h: (b, h)),
                   pl.BlockSpec((seq, hd), lambda b, h: (b, h))],
        out_shape=[jax.ShapeDtypeStruct((batch * ctx_len, RET_W), BF16),
                   jax.ShapeDtypeStruct((batch * seq, RET_W), BF16)],
        scratch_shapes=[
            pltpu.VMEM((2, hd, hd), F32),
            pltpu.VMEM((ctx_len, hd), F32), pltpu.VMEM((ctx_len, hd), F32),
            pltpu.VMEM((seq, hd), F32), pltpu.VMEM((seq, hd), F32),
        ],
        compiler_params=_params(("arbitrary", "arbitrary")),
        name="retention",
    )(a_f, a_b, zret_c, zret_c, zret_c, zret_c, zret_x, zret_x, zret_x, zret_x, norm_g.reshape(1, RET_W))


def _head_norm_rope(z, gain, cos, sin):
    ms = jnp.mean(z * z, axis=-1, keepdims=True)
    y = z * lax.rsqrt(ms + EPS) * gain
    if cos is None:
        return y
    return y * cos + pltpu.roll(y, HEAD_DIM // 2, 1) * sin


def _mid_kernel(x_ref, yhyt_ref, yret_ref, mod0_ref, mod1_ref, wo_hy_ref, wo_ret_ref, g1_ref, win_ref,
                qn_ref, kn_ref, *rest, latent):
    d = x_ref.shape[-1]
    if latent:
        cos_ref, sin_ref, xo_ref, q_ref, k_ref, v_ref, g_ref = rest
        cos, sin = cos_ref[...], sin_ref[...]
    else:
        k_ref, v_ref = rest
        cos = sin = None
    y = _dot_tn(yhyt_ref[...], wo_hy_ref[...]) + _dot(yret_ref[...], wo_ret_ref[...])
    x = x_ref[...] + mod0_ref[:, 2 * d:3 * d] * y
    if latent:
        xo_ref[...] = x
    mod1 = mod1_ref[...]
    h = _mod_norm(x, g1_ref[...], mod1[:, 0:d], mod1[:, d:2 * d]).astype(BF16)
    off = 0
    if latent:
        for hd in range(N_HEADS):
            z = _dot(h, win_ref[:, off:off + HEAD_DIM])
            q_ref[:, hd * HEAD_DIM:(hd + 1) * HEAD_DIM] = _head_norm_rope(z, qn_ref[...], cos, sin).astype(BF16)
            off += HEAD_DIM
    for hd in range(N_KV):
        z = _dot(h, win_ref[:, off:off + HEAD_DIM])
        k_ref[:, hd * HEAD_DIM:(hd + 1) * HEAD_DIM] = _head_norm_rope(z, kn_ref[...], cos, sin).astype(BF16)
        off += HEAD_DIM
    v_ref[...] = _dot(h, win_ref[:, off:off + KV_W]).astype(BF16)
    off += KV_W
    if latent:
        g_ref[...] = _dot(h, win_ref[:, off:off + ATT_W]).astype(BF16)


def _mid(x2, yhyt, yret, mod4, mod_row, wo_hy, wo_ret, g1, win, qn, kn, cosf, sinf, batch, seq, tm, latent):
    d = x2.shape[-1]
    nt = seq // tm
    tok = lambda w: pl.BlockSpec((tm, w), lambda b, t: (b * nt + t, 0))
    const = lambda a: pl.BlockSpec(a.shape, lambda b, t: (0,) * a.ndim)
    modspec = lambda layer: pl.BlockSpec((None, None, 1, 3 * d), lambda b, t: (layer, mod_row(b), 0, 0))
    in_specs = [tok(d), pl.BlockSpec((HY_W, tm), lambda b, t: (0, b * nt + t)), tok(RET_W),
                modspec(0), modspec(1), const(wo_hy), const(wo_ret), const(g1), const(win), const(qn), const(kn)]
    args = [x2, yhyt, yret, mod4, mod4, wo_hy, wo_ret, g1, win, qn, kn]
    n = batch * seq
    if latent:
        in_specs += [pl.BlockSpec((tm, HEAD_DIM), lambda b, t: (t, 0))] * 2
        args += [cosf, sinf]
        out_specs = [tok(d), tok(ATT_W), tok(KV_W), tok(KV_W), tok(ATT_W)]
        out_shape = [jax.ShapeDtypeStruct((n, d), F32), jax.ShapeDtypeStruct((n, ATT_W), BF16),
                     jax.ShapeDtypeStruct((n, KV_W), BF16), jax.ShapeDtypeStruct((n, KV_W), BF16),
                     jax.ShapeDtypeStruct((n, ATT_W), BF16)]
    else:
        out_specs = [tok(KV_W), tok(KV_W)]
        out_shape = [jax.ShapeDtypeStruct((n, KV_W), BF16), jax.ShapeDtypeStruct((n, KV_W), BF16)]
    return pl.pallas_call(
        functools.partial(_mid_kernel, latent=latent),
        grid=(batch, nt),
        in_specs=in_specs,
        out_specs=out_specs,
        out_shape=out_shape,
        compiler_params=_params(("arbitrary", "arbitrary")),
        name="mid_latent" if latent else "mid_ctx",
    )(*args)


def _attn_kernel(q_ref, kc_ref, vc_ref, kx_ref, vx_ref, o_ref):
    scale = HEAD_DIM ** -0.5
    kc, vc, kx, vx = kc_ref[...], vc_ref[...], kx_ref[...], vx_ref[...]
    for g in range(GROUP):
        sl = slice(g * HEAD_DIM, (g + 1) * HEAD_DIM)
        q = q_ref[:, sl]
        s_c = _dot_nt(q, kc) * scale
        s_x = _dot_nt(q, kx) * scale
        m = jnp.maximum(jnp.max(s_c, axis=-1, keepdims=True), jnp.max(s_x, axis=-1, keepdims=True))
        p_c = jnp.exp(s_c - m)
        p_x = jnp.exp(s_x - m)
        l = jnp.sum(p_c, axis=-1, keepdims=True) + jnp.sum(p_x, axis=-1, keepdims=True)
        o = _dot(p_c.astype(BF16), vc) + _dot(p_x.astype(BF16), vx)
        o_ref[:, sl] = (o / l).astype(BF16)


def _attention(q, kc, vc, kx, vx, batch, ctx_len, seq, tq):
    nt = seq // tq
    gw = GROUP * HEAD_DIM
    kv = lambda rows: pl.BlockSpec((rows, HEAD_DIM), lambda b, h, t: (b, h))
    return pl.pallas_call(
        _attn_kernel,
        grid=(batch, N_KV, nt),
        in_specs=[pl.BlockSpec((tq, gw), lambda b, h, t: (b * nt + t, h)),
                  kv(ctx_len), kv(ctx_len), kv(seq), kv(seq)],
        out_specs=pl.BlockSpec((tq, gw), lambda b, h, t: (b * nt + t, h)),
        out_shape=jax.ShapeDtypeStruct((batch * seq, ATT_W), BF16),
        compiler_params=_params(("arbitrary", "arbitrary", "arbitrary")),
        name="attention",
    )(q, kc, vc, kx, vx)


def _final_kernel(x_ref, o_ref, g_ref, mod_ref, w_ref, fg_ref, out_ref):
    d = x_ref.shape[-1]
    a = (_silu(g_ref[...].astype(F32)) * o_ref[...].astype(F32)).astype(BF16)
    x = x_ref[...] + mod_ref[:, 2 * d:3 * d] * _dot(a, w_ref[...])
    ms = jnp.mean(x * x, axis=-1, keepdims=True)
    out_ref[...] = x * lax.rsqrt(ms + EPS) * fg_ref[...]


def _final(x2, o, g, mod4, w, fg, batch, seq, tm):
    d = x2.shape[-1]
    nt = seq // tm
    tok = lambda w_: pl.BlockSpec((tm, w_), lambda b, t: (b * nt + t, 0))
    const = lambda a: pl.BlockSpec(a.shape, lambda b, t: (0,) * a.ndim)
    return pl.pallas_call(
        _final_kernel,
        grid=(batch, nt),
        in_specs=[tok(d), tok(ATT_W), tok(ATT_W),
                  pl.BlockSpec((None, None, 1, 3 * d), lambda b, t: (1, b, 0, 0)), const(w), const(fg)],
        out_specs=tok(d),
        out_shape=jax.ShapeDtypeStruct((batch * seq, d), F32),
        compiler_params=_params(("arbitrary", "arbitrary")),
        name="final",
    )(x2, o, g, mod4, w, fg)


def _rope_tables(seq):
    rows = seq // GRID_W
    r = jnp.repeat(jnp.arange(rows, dtype=F32), GRID_W)
    col = jnp.tile(jnp.arange(GRID_W, dtype=F32), rows)
    half = HEAD_DIM // 2
    inv = ROPE_THETA ** (-jnp.arange(0, half, 2, dtype=F32) / half)
    ang = jnp.concatenate([r[:, None] * inv, col[:, None] * inv], axis=-1)
    cos, sin = jnp.cos(ang), jnp.sin(ang)
    return jnp.concatenate([cos, cos], axis=-1), jnp.concatenate([-sin, sin], axis=-1)


def kernel(x, c, ctx, c_ctx, norm_g, ada_w, ada_b, er_in_w, er_out_w, hy_conv_w, hy_conv_b, hy_f_w1, hy_f_b1, hy_f_freq, hy_f_w2, hy_f_b2, hy_f_w3, hy_bias, ret_decay_f, ret_decay_b, ret_norm_g, at_in_w, at_out_w, at_q_norm, at_k_norm, final_norm_g):
    batch, seq, d = x.shape
    ctx_len = ctx.shape[1]
    assert seq % TOEP == 0 and ctx_len % TOEP == 0 and seq % GRID_W == 0
    tm = min(512, seq)

    rows = -(-(batch + 1) // 8) * 8
    cond = jnp.concatenate([c, c_ctx[None, :], jnp.zeros((rows - batch - 1, d), F32)], axis=0)
    mod4 = _ada(cond, ada_w, ada_b).reshape(ada_w.shape[0], rows, 1, 3 * d)
    lat_row = lambda b: b
    ctx_row = lambda b: batch

    x2 = x.reshape(batch * seq, d)
    c2 = ctx.reshape(batch * ctx_len, d)

    in_w = er_in_w[0]
    whyt = in_w[:, :4 * HY_W].T.astype(BF16)
    wret = in_w[:, 4 * HY_W:].astype(BF16)
    g0 = norm_g[0].reshape(1, d)
    zhyt_x, zret_x = _inproj0(x2, mod4, 0, lat_row, g0, whyt, wret, batch, seq, tm)
    zhyt_c, zret_c = _inproj0(c2, mod4, 0, ctx_row, g0, whyt, wret, batch, ctx_len, ctx_len)

    filt = (hy_f_w1[0], hy_f_b1[0], hy_f_freq[0], hy_f_w2[0], hy_f_b2[0], hy_f_w3[0], hy_bias[0])
    yhyt_x = _hyena(zhyt_x, hy_conv_w[0], hy_conv_b[0], _filters(seq, *filt), batch, seq)
    yhyt_c = _hyena(zhyt_c, hy_conv_w[0], hy_conv_b[0], _filters(ctx_len, *filt), batch, ctx_len)
    yret_c, yret_x = _retention(zret_c, zret_x, ret_decay_f[0], ret_decay_b[0], ret_norm_g[0], batch, ctx_len, seq)

    wo_hy = er_out_w[0][:HY_W].astype(BF16)
    wo_ret = er_out_w[0][HY_W:].astype(BF16)
    perm = jnp.concatenate([jnp.arange(0, HEAD_DIM, 2), jnp.arange(1, HEAD_DIM, 2)])
    qk_cols = (jnp.arange(N_HEADS + N_KV)[:, None] * HEAD_DIM + perm[None, :]).reshape(-1)
    w1 = at_in_w[0]
    win_x = jnp.concatenate([w1[:, qk_cols], w1[:, ATT_W + KV_W:]], axis=1).astype(BF16)
    win_c = win_x[:, ATT_W:ATT_W + 2 * KV_W]
    qn = at_q_norm[0][perm].reshape(1, HEAD_DIM)
    kn = at_k_norm[0][perm].reshape(1, HEAD_DIM)
    g1 = norm_g[1].reshape(1, d)
    cosf, sinf = _rope_tables(seq)
    x1, q, kx, vx, gate = _mid(x2, yhyt_x.reshape(HY_W, batch * seq), yret_x, mod4, lat_row, wo_hy, wo_ret, g1,
                               win_x, qn, kn, cosf, sinf, batch, seq, tm, True)
    kc, vc = _mid(c2, yhyt_c.reshape(HY_W, batch * ctx_len), yret_c, mod4, ctx_row, wo_hy, wo_ret, g1,
                  win_c, qn, kn, None, None, batch, ctx_len, ctx_len, False)

    o = _attention(q, kc, vc, kx, vx, batch, ctx_len, seq, min(256, seq))
    out = _final(x1, o, gate, mod4, at_out_w[0].astype(BF16), final_norm_g.reshape(1, d), batch, seq, tm)
    return out.reshape(batch, seq, d)
```

```python
import functools
import math

import jax
import jax.numpy as jnp
from jax import lax
from jax.experimental import pallas as pl
from jax.experimental.pallas import tpu as pltpu

F32 = jnp.float32
BF16 = jnp.bfloat16
HIGHEST = lax.Precision.HIGHEST

EPS = 1e-6
GRID_W = 64
HY_W = 512
HY_EMB = 33
HY_BANDS = (HY_EMB - 1) // 2
HY_ORDER = 64
HY_DECAY_TARGET = 1e-2
HY_FAST_PCT = 0.3
HY_SLOW_PCT = 1.5
RET_W = 512
RET_H = 4
RET_HD = RET_W // RET_H
CHUNK = 128
N_HEADS = 8
N_KV = 2
HEAD_DIM = 128
GROUP = N_HEADS // N_KV
ATT_W = N_HEADS * HEAD_DIM
KV_W = N_KV * HEAD_DIM
ROPE_THETA = 10000.0
Q_PRESCALE = HEAD_DIM ** -0.5 * math.log2(math.e)

LANES = 128
TOEP = 256
VMEM_LIMIT = 56 * 1024 * 1024
T_DEC, T_QDF, T_QDB, T_KDF, T_KDB, T_CDF, T_CDB = range(7)
N_TABLES = 7


def _params(sem):
    return pltpu.CompilerParams(dimension_semantics=sem, vmem_limit_bytes=VMEM_LIMIT)


def _silu(x):
    return x * (1.0 / (1.0 + jnp.exp(-x)))


def _dot(a, b):
    return jnp.dot(a, b, preferred_element_type=F32)


def _dot_nt(a, b):
    return lax.dot_general(a, b, (((1,), (1,)), ((), ())), preferred_element_type=F32)


def _dot_tn(a, b):
    return lax.dot_general(a, b, (((0,), (0,)), ((), ())), preferred_element_type=F32)


def _mod_norm(x, g, shift, scale):
    ms = jnp.mean(x * x, axis=-1, keepdims=True)
    return (x * lax.rsqrt(ms + EPS) * g) * (1.0 + scale) + shift


def _ada_kernel(cond_ref, w_ref, b_ref, o_ref):
    s = _silu(cond_ref[...])
    o_ref[...] = jnp.dot(s, w_ref[...], preferred_element_type=F32, precision=HIGHEST) + b_ref[...]


def _ada(cond, ada_w, ada_b):
    depth, d, n = ada_w.shape
    rows = cond.shape[0]
    tn = 1024
    return pl.pallas_call(
        _ada_kernel,
        grid=(depth, n // tn),
        in_specs=[
            pl.BlockSpec((rows, d), lambda i, j: (0, 0)),
            pl.BlockSpec((None, d, tn), lambda i, j: (i, 0, j)),
            pl.BlockSpec((None, 1, tn), lambda i, j: (i, 0, j)),
        ],
        out_specs=pl.BlockSpec((None, rows, tn), lambda i, j: (i, 0, j)),
        out_shape=jax.ShapeDtypeStruct((depth, rows, n), F32),
        compiler_params=_params(("arbitrary", "arbitrary")),
        name="ada",
    )(cond, ada_w, ada_b.reshape(depth, 1, n))


def _inproj0_kernel(x_ref, mod_ref, g_ref, whyt_ref, wret_ref, zhyt_ref, zret_ref):
    d = x_ref.shape[-1]
    mod = mod_ref[...]
    h = _mod_norm(x_ref[...], g_ref[...], mod[:, 0:d], mod[:, d:2 * d]).astype(BF16)
    nh = whyt_ref.shape[0]
    step = 512
    for n in range(0, nh, step):
        zhyt_ref[n:n + step, :] = _dot_nt(whyt_ref[n:n + step, :], h).astype(BF16)
    nr = wret_ref.shape[1]
    for n in range(0, nr, step):
        zret_ref[:, n:n + step] = _dot(h, wret_ref[:, n:n + step]).astype(BF16)


def _inproj0(x2, mod4, layer, mod_row, g, whyt, wret, batch, seq, tm):
    d = x2.shape[-1]
    nt = seq // tm
    nh, nr = whyt.shape[0], wret.shape[1]
    return pl.pallas_call(
        _inproj0_kernel,
        grid=(batch, nt),
        in_specs=[
            pl.BlockSpec((tm, d), lambda b, t: (b * nt + t, 0)),
            pl.BlockSpec((None, None, 1, 3 * d), lambda b, t: (layer, mod_row(b), 0, 0)),
            pl.BlockSpec((1, d), lambda b, t: (0, 0)),
            pl.BlockSpec((nh, d), lambda b, t: (0, 0)),
            pl.BlockSpec((d, nr), lambda b, t: (0, 0)),
        ],
        out_specs=[
            pl.BlockSpec((nh, tm), lambda b, t: (0, b * nt + t)),
            pl.BlockSpec((tm, nr), lambda b, t: (b * nt + t, 0)),
        ],
        out_shape=[
            jax.ShapeDtypeStruct((nh, batch * seq), BF16),
            jax.ShapeDtypeStruct((batch * seq, nr), BF16),
        ],
        compiler_params=_params(("arbitrary", "arbitrary")),
        name="inproj0",
    )(x2, mod4, g, whyt, wret)


def _filter_kernel(w1t_ref, w1c_ref, w1s_ref, b1_ref, f_ref, w2_ref, b2_ref, w3_ref, bias_ref, o_ref, *, seq):
    two_l = o_ref.shape[-1]
    cb = o_ref.shape[0]
    lag = lax.broadcasted_iota(jnp.int32, (1, two_l), 1) - seq
    n = jnp.abs(lag).astype(F32)
    t = n / float(max(seq - 1, 1))
    band_i = lax.broadcasted_iota(jnp.int32, (HY_BANDS, 1), 0).astype(F32)
    bands = 1e-4 + band_i * ((HY_BANDS - 1 - 1e-4) / (HY_BANDS - 1))
    ang = (2.0 * math.pi / seq) * n * bands
    f = f_ref[...]
    pre = (w1t_ref[...] * t
           + jnp.dot(w1c_ref[...], jnp.cos(ang), preferred_element_type=F32, precision=HIGHEST)
           + jnp.dot(w1s_ref[...], -jnp.sin(ang), preferred_element_type=F32, precision=HIGHEST))
    h = jnp.sin(f * (pre + b1_ref[...]))
    h = jnp.sin(f * (jnp.dot(w2_ref[...], h, preferred_element_type=F32, precision=HIGHEST) + b2_ref[...]))
    h_f = jnp.dot(w3_ref[0], h, preferred_element_type=F32, precision=HIGHEST)
    h_b = jnp.dot(w3_ref[1], h, preferred_element_type=F32, precision=HIGHEST)
    max_decay = math.log(HY_DECAY_TARGET) / HY_FAST_PCT
    min_decay = math.log(HY_DECAY_TARGET) / HY_SLOW_PCT
    ch = (lax.broadcasted_iota(jnp.int32, (cb, 1), 0) + pl.program_id(0) * cb).astype(F32)
    deltas = jnp.abs(min_decay + ch * ((max_decay - min_decay) / (HY_W - 1)))
    window = jnp.exp(-t * deltas)
    k = jnp.where(lag >= 0, h_f, h_b) * window
    k = jnp.where(lag == -seq, 0.0, k)
    o_ref[...] = k + jnp.where(lag == 0, bias_ref[...], 0.0)


def _filters(seq, w1, b1, freq, w2, b2, w3, bias):
    cb = 128
    w1t = w1.T
    args = (
        w1t[:, 0:1], w1t[:, 1:1 + HY_BANDS], w1t[:, 1 + HY_BANDS:],
        b1.reshape(HY_ORDER, 1), freq.reshape(HY_ORDER, 1), w2.T, b2.reshape(HY_ORDER, 1),
        w3.T.reshape(2, HY_W, HY_ORDER), bias.reshape(HY_W, 1),
    )
    full = lambda a: pl.BlockSpec(a.shape, lambda i: (0,) * a.ndim)
    in_specs = [full(a) for a in args[:7]] + [
        pl.BlockSpec((2, cb, HY_ORDER), lambda i: (0, i, 0)),
        pl.BlockSpec((cb, 1), lambda i: (i, 0)),
    ]
    return pl.pallas_call(
        functools.partial(_filter_kernel, seq=seq),
        grid=(HY_W // cb,),
        in_specs=in_specs,
        out_specs=pl.BlockSpec((cb, 2 * seq), lambda i: (i, 0)),
        out_shape=jax.ShapeDtypeStruct((HY_W, 2 * seq), F32),
        compiler_params=_params(("arbitrary",)),
        name="hyena_filters",
    )(*args)


def _hyena_kernel(cw_ref, cb_ref, x0_ref, x1_ref, v_ref, gate_ref, kext_ref, out_ref,
                  r_ref, t_ref, u2_ref, y2_ref, *, seq, batch):
    cblk = out_ref.shape[0]
    nb = seq // TOEP
    nq = 2 * seq // LANES
    half = seq // LANES
    lane = lax.broadcasted_iota(jnp.int32, (1, seq), 1)
    first = lane == 0
    last = lane == seq - 1
    row_i = lax.broadcasted_iota(jnp.int32, (LANES, LANES), 0)
    col_i = lax.broadcasted_iota(jnp.int32, (LANES, LANES), 1)
    upper = col_i >= row_i
    c0 = pl.program_id(0) * cblk

    def short_conv(ref, c, part):
        a = ref[c].astype(F32)
        ch = part * HY_W + c0 + c
        left = jnp.where(first, 0.0, pltpu.roll(a, 1, 1))
        right = jnp.where(last, 0.0, pltpu.roll(a, seq - 1, 1))
        return left * cw_ref[0, ch] + a * cw_ref[1, ch] + right * cw_ref[2, ch] + cb_ref[ch]

    def channel(c, carry):
        krow = kext_ref[pl.ds(c, 1), :]
        for q in range(nq):
            a = jnp.broadcast_to(krow[:, q * LANES:(q + 1) * LANES], (LANES, LANES))
            r_ref[q] = pltpu.roll(a, 0, 1, stride=1, stride_axis=0)
        for e in range(-(2 * nb - 1), 2 * nb):
            q0 = half + e
            g = jnp.where(upper, r_ref[q0], r_ref[q0 - 1]).astype(BF16)
            if e % 2 == 0:
                d = e // 2
                t_ref[d + nb - 1, 0:LANES, 0:LANES] = g
                t_ref[d + nb - 1, LANES:TOEP, LANES:TOEP] = g
            else:
                d = (e - 1) // 2
                if abs(d) <= nb - 1:
                    t_ref[d + nb - 1, 0:LANES, LANES:TOEP] = g
                d = (e + 1) // 2
                if abs(d) <= nb - 1:
                    t_ref[d + nb - 1, LANES:TOEP, 0:LANES] = g
        u = (short_conv(x1_ref, c, 1) * short_conv(v_ref, c, 2)).astype(BF16)
        for j in range(nb):
            u2_ref[j * batch:(j + 1) * batch, :] = u[:, j * TOEP:(j + 1) * TOEP]
        y2_ref[...] = _dot(u2_ref[...], t_ref[nb - 1])
        for d in list(range(1, nb)) + list(range(-(nb - 1), 0)):
            j_lo, j_hi = max(0, -d), nb - max(0, d)
            res = _dot(u2_ref[j_lo * batch:j_hi * batch, :], t_ref[d + nb - 1])
            y2_ref[(j_lo + d) * batch:(j_hi + d) * batch, :] += res
        x0 = short_conv(x0_ref, c, 0)
        gate = _silu(gate_ref[c].astype(F32))
        for j in range(nb):
            sl = slice(j * TOEP, (j + 1) * TOEP)
            out_ref[c, :, sl] = (x0[:, sl] * y2_ref[j * batch:(j + 1) * batch, :] * gate[:, sl]).astype(BF16)
        return carry

    lax.fori_loop(0, cblk, channel, 0)


def _hyena(zhyt, conv_w, conv_b, kext, batch, seq):
    cblk = 8
    nb = seq // TOEP
    z4 = zhyt.reshape(4, HY_W, batch, seq)
    part = lambda p: pl.BlockSpec((None, cblk, batch, seq), lambda i: (p, i, 0, 0))
    smem = pl.BlockSpec(memory_space=pltpu.SMEM)
    return pl.pallas_call(
        functools.partial(_hyena_kernel, seq=seq, batch=batch),
        grid=(HY_W // cblk,),
        in_specs=[smem, smem, part(0), part(1), part(2), part(3),
                  pl.BlockSpec((cblk, 2 * seq), lambda i: (i, 0))],
        out_specs=pl.BlockSpec((cblk, batch, seq), lambda i: (i, 0, 0)),
        out_shape=jax.ShapeDtypeStruct((HY_W, batch, seq), BF16),
        scratch_shapes=[
            pltpu.VMEM((2 * seq // LANES, LANES, LANES), F32),
            pltpu.VMEM((2 * nb - 1, TOEP, TOEP), BF16),
            pltpu.VMEM((nb * batch, TOEP), BF16),
            pltpu.VMEM((nb * batch, TOEP), F32),
        ],
        compiler_params=_params(("arbitrary",)),
        name="hyena",
    )(conv_w, conv_b, z4, z4, z4, z4, kext)


def _retention_tables(af_ref, ab_ref, tab_ref):
    k_scale = RET_HD ** -0.5
    ri = lax.broadcasted_iota(jnp.int32, (CHUNK, CHUNK), 0).astype(F32)
    ci = lax.broadcasted_iota(jnp.int32, (CHUNK, CHUNK), 1).astype(F32)
    diff = ri - ci
    for h in range(RET_H):
        log_f = -jnp.exp(jnp.full((CHUNK, CHUNK), af_ref[h], F32))
        log_b = -jnp.exp(jnp.full((CHUNK, CHUNK), ab_ref[h], F32))
        dec_f = jnp.where(diff >= 0, jnp.exp(log_f * jnp.maximum(diff, 0.0)), 0.0)
        dec_b = jnp.where(diff <= 0, jnp.exp(log_b * jnp.maximum(-diff, 0.0)), 0.0)
        tab_ref[h, T_DEC] = (dec_f + dec_b) * k_scale
        tab_ref[h, T_QDF] = jnp.exp(log_f * (ri + 1.0))
        tab_ref[h, T_QDB] = jnp.exp(log_b * (CHUNK - ri))
        tab_ref[h, T_KDF] = jnp.exp(log_f * (CHUNK - 1.0 - ri)) * k_scale
        tab_ref[h, T_KDB] = jnp.exp(log_b * ri) * k_scale
        tab_ref[h, T_CDF] = jnp.exp(log_f * CHUNK)
        tab_ref[h, T_CDB] = jnp.exp(log_b * CHUNK)


def _retention_kernel(af_ref, ab_ref, zc_ref, zx_ref, ng_ref, oc_ref, ox_ref,
                      tab_ref, sf_ref, sb_ref, sbc_ref, sbx_ref):
    hd = RET_HD

    @pl.when(pl.program_id(0) == 0)
    def _():
        _retention_tables(af_ref, ab_ref, tab_ref)

    sf_ref[...] = jnp.zeros_like(sf_ref)
    sb_ref[...] = jnp.zeros_like(sb_ref)

    def head_cols(part, h):
        return slice((part * RET_H + h) * hd, (part * RET_H + h + 1) * hd)

    def state_update(s, k, v, kd, cd):
        kdt = jnp.transpose(k.astype(F32) * kd).astype(BF16)
        return s * cd + _dot(kdt, v)

    def bwd_pass(z_ref, store_ref):
        nchunk = z_ref.shape[0] // CHUNK

        def body(i, carry):
            t = nchunk - 1 - i
            rows = pl.ds(pl.multiple_of(t * CHUNK, CHUNK), CHUNK)
            for h in range(RET_H):
                k = z_ref[rows, head_cols(1, h)]
                v = z_ref[rows, head_cols(2, h)]
                s = sb_ref[h]
                store_ref[t, h] = s.astype(BF16)
                sb_ref[h] = state_update(s, k, v, tab_ref[h, T_KDB], tab_ref[h, T_CDB])
            return carry

        lax.fori_loop(0, nchunk, body, 0)

    def fwd_pass(z_ref, store_ref, out_ref):
        nchunk = z_ref.shape[0] // CHUNK

        def body(t, carry):
            rows = pl.ds(pl.multiple_of(t * CHUNK, CHUNK), CHUNK)
            for h in range(RET_H):
                q = z_ref[rows, head_cols(0, h)]
                k = z_ref[rows, head_cols(1, h)]
                v = z_ref[rows, head_cols(2, h)]
                sf = sf_ref[h]
                scores = _dot_nt(q, k) * tab_ref[h, T_DEC]
                states = jnp.concatenate([sf.astype(BF16), store_ref[t, h]], axis=1)
                cross = _dot(q, states)
                o = (_dot(scores.astype(BF16), v) + cross[:, :hd] * tab_ref[h, T_QDF]
                     + cross[:, hd:] * tab_ref[h, T_QDB])
                sf_ref[h] = state_update(sf, k, v, tab_ref[h, T_KDF], tab_ref[h, T_CDF])
                mu = jnp.mean(o, axis=-1, keepdims=True)
                oc = o - mu
                var = jnp.mean(oc * oc, axis=-1, keepdims=True)
                y = oc * lax.rsqrt(var + EPS) * ng_ref[:, h * hd:(h + 1) * hd]
                gate = _silu(z_ref[rows, head_cols(3, h)].astype(F32))
                out_ref[rows, h * hd:(h + 1) * hd] = (y * gate).astype(BF16)
            return carry

        lax.fori_loop(0, nchunk, body, 0)

    bwd_pass(zc_ref, sbc_ref)
    bwd_pass(zx_ref, sbx_ref)
    fwd_pass(zc_ref, sbc_ref, oc_ref)
    fwd_pass(zx_ref, sbx_ref, ox_ref)


def _retention(zret_c, zret_x, a_f, a_b, norm_g, batch, ctx_len, seq):
    hd = RET_HD
    smem = pl.BlockSpec(memory_space=pltpu.SMEM)
    return pl.pallas_call(
        _retention_kernel,
        grid=(batch,),
        in_specs=[smem, smem,
                  pl.BlockSpec((ctx_len, 4 * RET_W), lambda b: (b, 0)),
                  pl.BlockSpec((seq, 4 * RET_W), lambda b: (b, 0)),
                  pl.BlockSpec((1, RET_W), lambda b: (0, 0))],
        out_specs=[pl.BlockSpec((ctx_len, RET_W), lambda b: (b, 0)),
                   pl.BlockSpec((seq, RET_W), lambda b: (b, 0))],
        out_shape=[jax.ShapeDtypeStruct((batch * ctx_len, RET_W), BF16),
                   jax.ShapeDtypeStruct((batch * seq, RET_W), BF16)],
        scratch_shapes=[
            pltpu.VMEM((RET_H, N_TABLES, CHUNK, CHUNK), F32),
            pltpu.VMEM((RET_H, hd, hd), F32),
            pltpu.VMEM((RET_H, hd, hd), F32),
            pltpu.VMEM((ctx_len // CHUNK, RET_H, hd, hd), BF16),
            pltpu.VMEM((seq // CHUNK, RET_H, hd, hd), BF16),
        ],
        compiler_params=_params(("arbitrary",)),
        name="retention",
    )(a_f, a_b, zret_c, zret_x, norm_g.reshape(1, RET_W))


def _head_norm_rope(z, gain, cos, sin):
    ms = jnp.mean(z * z, axis=-1, keepdims=True)
    y = z * lax.rsqrt(ms + EPS) * gain
    if cos is None:
        return y
    return y * cos + pltpu.roll(y, HEAD_DIM // 2, 1) * sin


def _store_v_aug(v_ref, v):
    ones = jnp.ones((v.shape[0], HEAD_DIM), BF16)
    for hd in range(N_KV):
        v_ref[:, 2 * hd * HEAD_DIM:(2 * hd + 1) * HEAD_DIM] = v[:, hd * HEAD_DIM:(hd + 1) * HEAD_DIM].astype(BF16)
        v_ref[:, (2 * hd + 1) * HEAD_DIM:(2 * hd + 2) * HEAD_DIM] = ones


def _mid_kernel(x_ref, yhyt_ref, yret_ref, mod0_ref, mod1_ref, wo_hy_ref, wo_ret_ref, g1_ref, win_ref,
                qn_ref, kn_ref, *rest, latent):
    d = x_ref.shape[-1]
    if latent:
        cos_ref, sin_ref, xo_ref, q_ref, k_ref, v_ref, g_ref = rest
        cos, sin = cos_ref[...], sin_ref[...]
    else:
        k_ref, v_ref = rest
        cos = sin = None
    y = _dot_tn(yhyt_ref[...], wo_hy_ref[...]) + _dot(yret_ref[...], wo_ret_ref[...])
    x = x_ref[...] + mod0_ref[:, 2 * d:3 * d] * y
    if latent:
        xo_ref[...] = x
    mod1 = mod1_ref[...]
    h = _mod_norm(x, g1_ref[...], mod1[:, 0:d], mod1[:, d:2 * d]).astype(BF16)
    off = 0
    if latent:
        for hd in range(N_HEADS):
            z = _dot(h, win_ref[:, off:off + HEAD_DIM])
            q = _head_norm_rope(z, qn_ref[...], cos, sin) * Q_PRESCALE
            q_ref[:, hd * HEAD_DIM:(hd + 1) * HEAD_DIM] = q.astype(BF16)
            off += HEAD_DIM
    for hd in range(N_KV):
        z = _dot(h, win_ref[:, off:off + HEAD_DIM])
        k_ref[:, hd * HEAD_DIM:(hd + 1) * HEAD_DIM] = _head_norm_rope(z, kn_ref[...], cos, sin).astype(BF16)
        off += HEAD_DIM
    _store_v_aug(v_ref, _dot(h, win_ref[:, off:off + KV_W]))
    off += KV_W
    if latent:
        g_ref[...] = _dot(h, win_ref[:, off:off + ATT_W]).astype(BF16)


def _mid(x2, yhyt, yret, mod4, mod_row, wo_hy, wo_ret, g1, win, qn, kn, cosf, sinf, batch, seq, tm, latent):
    d = x2.shape[-1]
    nt = seq // tm
    tok = lambda w: pl.BlockSpec((tm, w), lambda b, t: (b * nt + t, 0))
    const = lambda a: pl.BlockSpec(a.shape, lambda b, t: (0,) * a.ndim)
    modspec = lambda layer: pl.BlockSpec((None, None, 1, 3 * d), lambda b, t: (layer, mod_row(b), 0, 0))
    in_specs = [tok(d), pl.BlockSpec((HY_W, tm), lambda b, t: (0, b * nt + t)), tok(RET_W),
                modspec(0), modspec(1), const(wo_hy), const(wo_ret), const(g1), const(win), const(qn), const(kn)]
    args = [x2, yhyt, yret, mod4, mod4, wo_hy, wo_ret, g1, win, qn, kn]
    n = batch * seq
    if latent:
        in_specs += [pl.BlockSpec((tm, HEAD_DIM), lambda b, t: (t, 0))] * 2
        args += [cosf, sinf]
        out_specs = [tok(d), tok(ATT_W), tok(KV_W), tok(2 * KV_W), tok(ATT_W)]
        out_shape = [jax.ShapeDtypeStruct((n, d), F32), jax.ShapeDtypeStruct((n, ATT_W), BF16),
                     jax.ShapeDtypeStruct((n, KV_W), BF16), jax.ShapeDtypeStruct((n, 2 * KV_W), BF16),
                     jax.ShapeDtypeStruct((n, ATT_W), BF16)]
    else:
        out_specs = [tok(KV_W), tok(2 * KV_W)]
        out_shape = [jax.ShapeDtypeStruct((n, KV_W), BF16), jax.ShapeDtypeStruct((n, 2 * KV_W), BF16)]
    return pl.pallas_call(
        functools.partial(_mid_kernel, latent=latent),
        grid=(batch, nt),
        in_specs=in_specs,
        out_specs=out_specs,
        out_shape=out_shape,
        compiler_params=_params(("arbitrary", "arbitrary")),
        name="mid_latent" if latent else "mid_ctx",
    )(*args)


def _attn_kernel(q_ref, kc_ref, vc_ref, kx_ref, vx_ref, o_ref):
    kc, vc, kx, vx = kc_ref[...], vc_ref[...], kx_ref[...], vx_ref[...]
    for g in range(GROUP):
        sl = slice(g * HEAD_DIM, (g + 1) * HEAD_DIM)
        q = q_ref[:, sl]
        s_c = _dot_nt(q, kc)
        s_x = _dot_nt(q, kx)
        m = jnp.maximum(jnp.max(s_c, axis=-1, keepdims=True), jnp.max(s_x, axis=-1, keepdims=True))
        p_c = jnp.exp2(s_c - m).astype(BF16)
        p_x = jnp.exp2(s_x - m).astype(BF16)
        o = _dot(p_c, vc) + _dot(p_x, vx)
        o_ref[:, sl] = (o[:, :HEAD_DIM] / o[:, HEAD_DIM:]).astype(BF16)


def _attention(q, kc, vc, kx, vx, batch, ctx_len, seq, tq):
    nt = seq // tq
    gw = GROUP * HEAD_DIM
    kv = lambda rows, w=HEAD_DIM: pl.BlockSpec((rows, w), lambda b, h, t: (b, h))
    return pl.pallas_call(
        _attn_kernel,
        grid=(batch, N_KV, nt),
        in_specs=[pl.BlockSpec((tq, gw), lambda b, h, t: (b * nt + t, h)),
                  kv(ctx_len), kv(ctx_len, 2 * HEAD_DIM), kv(seq), kv(seq, 2 * HEAD_DIM)],
        out_specs=pl.BlockSpec((tq, gw), lambda b, h, t: (b * nt + t, h)),
        out_shape=jax.ShapeDtypeStruct((batch * seq, ATT_W), BF16),
        compiler_params=_params(("arbitrary", "arbitrary", "arbitrary")),
        name="attention",
    )(q, kc, vc, kx, vx)


def _final_kernel(x_ref, o_ref, g_ref, mod_ref, w_ref, fg_ref, out_ref):
    d = x_ref.shape[-1]
    a = (_silu(g_ref[...].astype(F32)) * o_ref[...].astype(F32)).astype(BF16)
    x = x_ref[...] + mod_ref[:, 2 * d:3 * d] * _dot(a, w_ref[...])
    ms = jnp.mean(x * x, axis=-1, keepdims=True)
    out_ref[...] = x * lax.rsqrt(ms + EPS) * fg_ref[...]


def _final(x2, o, g, mod4, w, fg, batch, seq, tm):
    d = x2.shape[-1]
    nt = seq // tm
    tok = lambda w_: pl.BlockSpec((tm, w_), lambda b, t: (b * nt + t, 0))
    const = lambda a: pl.BlockSpec(a.shape, lambda b, t: (0,) * a.ndim)
    return pl.pallas_call(
        _final_kernel,
        grid=(batch, nt),
        in_specs=[tok(d), tok(ATT_W), tok(ATT_W),
                  pl.BlockSpec((None, None, 1, 3 * d), lambda b, t: (1, b, 0, 0)), const(w), const(fg)],
        out_specs=tok(d),
        out_shape=jax.ShapeDtypeStruct((batch * seq, d), F32),
        compiler_params=_params(("arbitrary", "arbitrary")),
        name="final",
    )(x2, o, g, mod4, w, fg)


def _rope_tables(seq):
    rows = seq // GRID_W
    r = jnp.repeat(jnp.arange(rows, dtype=F32), GRID_W)
    col = jnp.tile(jnp.arange(GRID_W, dtype=F32), rows)
    half = HEAD_DIM // 2
    inv = ROPE_THETA ** (-jnp.arange(0, half, 2, dtype=F32) / half)
    ang = jnp.concatenate([r[:, None] * inv, col[:, None] * inv], axis=-1)
    cos, sin = jnp.cos(ang), jnp.sin(ang)
    return jnp.concatenate([cos, cos], axis=-1), jnp.concatenate([-sin, sin], axis=-1)


def kernel(x, c, ctx, c_ctx, norm_g, ada_w, ada_b, er_in_w, er_out_w, hy_conv_w, hy_conv_b, hy_f_w1, hy_f_b1, hy_f_freq, hy_f_w2, hy_f_b2, hy_f_w3, hy_bias, ret_decay_f, ret_decay_b, ret_norm_g, at_in_w, at_out_w, at_q_norm, at_k_norm, final_norm_g):
    batch, seq, d = x.shape
    ctx_len = ctx.shape[1]
    assert seq % TOEP == 0 and ctx_len % TOEP == 0 and seq % GRID_W == 0
    tm = min(512, seq)

    rows = -(-(batch + 1) // 8) * 8
    cond = jnp.concatenate([c, c_ctx[None, :], jnp.zeros((rows - batch - 1, d), F32)], axis=0)
    mod4 = _ada(cond, ada_w, ada_b).reshape(ada_w.shape[0], rows, 1, 3 * d)
    lat_row = lambda b: b
    ctx_row = lambda b: batch

    x2 = x.reshape(batch * seq, d)
    c2 = ctx.reshape(batch * ctx_len, d)

    in_w = er_in_w[0]
    whyt = in_w[:, :4 * HY_W].T.astype(BF16)
    wret = in_w[:, 4 * HY_W:].astype(BF16)
    g0 = norm_g[0].reshape(1, d)
    zhyt_x, zret_x = _inproj0(x2, mod4, 0, lat_row, g0, whyt, wret, batch, seq, tm)
    zhyt_c, zret_c = _inproj0(c2, mod4, 0, ctx_row, g0, whyt, wret, batch, ctx_len, ctx_len)

    filt = (hy_f_w1[0], hy_f_b1[0], hy_f_freq[0], hy_f_w2[0], hy_f_b2[0], hy_f_w3[0], hy_bias[0])
    yhyt_x = _hyena(zhyt_x, hy_conv_w[0], hy_conv_b[0], _filters(seq, *filt), batch, seq)
    yhyt_c = _hyena(zhyt_c, hy_conv_w[0], hy_conv_b[0], _filters(ctx_len, *filt), batch, ctx_len)
    yret_c, yret_x = _retention(zret_c, zret_x, ret_decay_f[0], ret_decay_b[0], ret_norm_g[0], batch, ctx_len, seq)

    wo_hy = er_out_w[0][:HY_W].astype(BF16)
    wo_ret = er_out_w[0][HY_W:].astype(BF16)
    perm = jnp.concatenate([jnp.arange(0, HEAD_DIM, 2), jnp.arange(1, HEAD_DIM, 2)])
    qk_cols = (jnp.arange(N_HEADS + N_KV)[:, None] * HEAD_DIM + perm[None, :]).reshape(-1)
    w1 = at_in_w[0]
    win_x = jnp.concatenate([w1[:, qk_cols], w1[:, ATT_W + KV_W:]], axis=1).astype(BF16)
    win_c = win_x[:, ATT_W:ATT_W + 2 * KV_W]
    qn = at_q_norm[0][perm].reshape(1, HEAD_DIM)
    kn = at_k_norm[0][perm].reshape(1, HEAD_DIM)
    g1 = norm_g[1].reshape(1, d)
    cosf, sinf = _rope_tables(seq)
    x1, q, kx, vx, gate = _mid(x2, yhyt_x.reshape(HY_W, batch * seq), yret_x, mod4, lat_row, wo_hy, wo_ret, g1,
                               win_x, qn, kn, cosf, sinf, batch, seq, tm, True)
    kc, vc = _mid(c2, yhyt_c.reshape(HY_W, batch * ctx_len), yret_c, mod4, ctx_row, wo_hy, wo_ret, g1,
                  win_c, qn, kn, None, None, batch, ctx_len, ctx_len, False)

    o = _attention(q, kc, vc, kx, vx, batch, ctx_len, seq, min(256, seq))
    out = _final(x1, o, gate, mod4, at_out_w[0].astype(BF16), final_norm_g.reshape(1, d), batch, seq, tm)
    return out.reshape(batch, seq, d)
```

```python
import functools
import math

import jax
import jax.numpy as jnp
from jax import lax
from jax.experimental import pallas as pl
from jax.experimental.pallas import tpu as pltpu

F32 = jnp.float32
BF16 = jnp.bfloat16
HIGHEST = lax.Precision.HIGHEST

EPS = 1e-6
GRID_W = 64
HY_W = 512
HY_EMB = 33
HY_BANDS = (HY_EMB - 1) // 2
HY_ORDER = 64
HY_DECAY_TARGET = 1e-2
HY_FAST_PCT = 0.3
HY_SLOW_PCT = 1.5
RET_W = 512
RET_H = 4
RET_HD = RET_W // RET_H
CHUNK = 128
N_HEADS = 8
N_KV = 2
HEAD_DIM = 128
GROUP = N_HEADS // N_KV
ATT_W = N_HEADS * HEAD_DIM
KV_W = N_KV * HEAD_DIM
ROPE_THETA = 10000.0
Q_PRESCALE = HEAD_DIM ** -0.5 * math.log2(math.e)

MID_ROWS = 256
LANES = 128
TOEP = 256
VMEM_LIMIT = 56 * 1024 * 1024
T_DEC, T_QDF, T_QDB, T_KDF, T_KDB, T_CDF, T_CDB = range(7)
N_TABLES = 7


def _params(sem):
    return pltpu.CompilerParams(dimension_semantics=sem, vmem_limit_bytes=VMEM_LIMIT)


def _silu(x):
    return x * (1.0 / (1.0 + jnp.exp(-x)))


def _dot(a, b):
    return jnp.dot(a, b, preferred_element_type=F32)


def _dot_nt(a, b):
    return lax.dot_general(a, b, (((1,), (1,)), ((), ())), preferred_element_type=F32)


def _dot_tn(a, b):
    return lax.dot_general(a, b, (((0,), (0,)), ((), ())), preferred_element_type=F32)


def _mod_norm(x, g, shift, scale):
    ms = jnp.mean(x * x, axis=-1, keepdims=True)
    return (x * lax.rsqrt(ms + EPS) * g) * (1.0 + scale) + shift


def _ada_kernel(cond_ref, w_ref, b_ref, o_ref):
    s = _silu(cond_ref[...])
    o_ref[...] = jnp.dot(s, w_ref[...], preferred_element_type=F32, precision=HIGHEST) + b_ref[...]


def _ada(cond, ada_w, ada_b):
    depth, d, n = ada_w.shape
    rows = cond.shape[0]
    tn = 1024
    return pl.pallas_call(
        _ada_kernel,
        grid=(depth, n // tn),
        in_specs=[
            pl.BlockSpec((rows, d), lambda i, j: (0, 0)),
            pl.BlockSpec((None, d, tn), lambda i, j: (i, 0, j)),
            pl.BlockSpec((None, 1, tn), lambda i, j: (i, 0, j)),
        ],
        out_specs=pl.BlockSpec((None, rows, tn), lambda i, j: (i, 0, j)),
        out_shape=jax.ShapeDtypeStruct((depth, rows, n), F32),
        compiler_params=_params(("arbitrary", "arbitrary")),
        name="ada",
    )(cond, ada_w, ada_b.reshape(depth, 1, n))


def _inproj0_kernel(x_ref, mod_ref, g_ref, whyt_ref, wret_ref, zhyt_ref, zret_ref):
    d = x_ref.shape[-1]
    mod = mod_ref[...]
    h = _mod_norm(x_ref[...], g_ref[...], mod[:, 0:d], mod[:, d:2 * d]).astype(BF16)
    nh = whyt_ref.shape[0]
    step = 512
    for n in range(0, nh, step):
        zhyt_ref[n:n + step, :] = _dot_nt(whyt_ref[n:n + step, :], h).astype(BF16)
    nr = wret_ref.shape[1]
    for n in range(0, nr, step):
        zret_ref[:, n:n + step] = _dot(h, wret_ref[:, n:n + step]).astype(BF16)


def _inproj0(x2, mod4, layer, mod_row, g, whyt, wret, batch, seq, tm):
    d = x2.shape[-1]
    nt = seq // tm
    nh, nr = whyt.shape[0], wret.shape[1]
    return pl.pallas_call(
        _inproj0_kernel,
        grid=(batch, nt),
        in_specs=[
            pl.BlockSpec((tm, d), lambda b, t: (b * nt + t, 0)),
            pl.BlockSpec((None, None, 1, 3 * d), lambda b, t: (layer, mod_row(b), 0, 0)),
            pl.BlockSpec((1, d), lambda b, t: (0, 0)),
            pl.BlockSpec((nh, d), lambda b, t: (0, 0)),
            pl.BlockSpec((d, nr), lambda b, t: (0, 0)),
        ],
        out_specs=[
            pl.BlockSpec((nh, tm), lambda b, t: (0, b * nt + t)),
            pl.BlockSpec((tm, nr), lambda b, t: (b * nt + t, 0)),
        ],
        out_shape=[
            jax.ShapeDtypeStruct((nh, batch * seq), BF16),
            jax.ShapeDtypeStruct((batch * seq, nr), BF16),
        ],
        compiler_params=_params(("arbitrary", "arbitrary")),
        name="inproj0",
    )(x2, mod4, g, whyt, wret)


def _filter_kernel(w1t_ref, w1c_ref, w1s_ref, b1_ref, f_ref, w2_ref, b2_ref, w3_ref, bias_ref, o_ref, *, seq):
    two_l = o_ref.shape[-1]
    cb = o_ref.shape[0]
    lag = lax.broadcasted_iota(jnp.int32, (1, two_l), 1) - seq
    n = jnp.abs(lag).astype(F32)
    t = n / float(max(seq - 1, 1))
    band_i = lax.broadcasted_iota(jnp.int32, (HY_BANDS, 1), 0).astype(F32)
    bands = 1e-4 + band_i * ((HY_BANDS - 1 - 1e-4) / (HY_BANDS - 1))
    ang = (2.0 * math.pi / seq) * n * bands
    f = f_ref[...]
    pre = (w1t_ref[...] * t
           + jnp.dot(w1c_ref[...], jnp.cos(ang), preferred_element_type=F32, precision=HIGHEST)
           + jnp.dot(w1s_ref[...], -jnp.sin(ang), preferred_element_type=F32, precision=HIGHEST))
    h = jnp.sin(f * (pre + b1_ref[...]))
    h = jnp.sin(f * (jnp.dot(w2_ref[...], h, preferred_element_type=F32, precision=HIGHEST) + b2_ref[...]))
    h_f = jnp.dot(w3_ref[0], h, preferred_element_type=F32, precision=HIGHEST)
    h_b = jnp.dot(w3_ref[1], h, preferred_element_type=F32, precision=HIGHEST)
    max_decay = math.log(HY_DECAY_TARGET) / HY_FAST_PCT
    min_decay = math.log(HY_DECAY_TARGET) / HY_SLOW_PCT
    ch = (lax.broadcasted_iota(jnp.int32, (cb, 1), 0) + pl.program_id(0) * cb).astype(F32)
    deltas = jnp.abs(min_decay + ch * ((max_decay - min_decay) / (HY_W - 1)))
    window = jnp.exp(-t * deltas)
    k = jnp.where(lag >= 0, h_f, h_b) * window
    k = jnp.where(lag == -seq, 0.0, k)
    o_ref[...] = k + jnp.where(lag == 0, bias_ref[...], 0.0)


def _filters(seq, w1, b1, freq, w2, b2, w3, bias):
    cb = 128
    w1t = w1.T
    args = (
        w1t[:, 0:1], w1t[:, 1:1 + HY_BANDS], w1t[:, 1 + HY_BANDS:],
        b1.reshape(HY_ORDER, 1), freq.reshape(HY_ORDER, 1), w2.T, b2.reshape(HY_ORDER, 1),
        w3.T.reshape(2, HY_W, HY_ORDER), bias.reshape(HY_W, 1),
    )
    full = lambda a: pl.BlockSpec(a.shape, lambda i: (0,) * a.ndim)
    in_specs = [full(a) for a in args[:7]] + [
        pl.BlockSpec((2, cb, HY_ORDER), lambda i: (0, i, 0)),
        pl.BlockSpec((cb, 1), lambda i: (i, 0)),
    ]
    return pl.pallas_call(
        functools.partial(_filter_kernel, seq=seq),
        grid=(HY_W // cb,),
        in_specs=in_specs,
        out_specs=pl.BlockSpec((cb, 2 * seq), lambda i: (i, 0)),
        out_shape=jax.ShapeDtypeStruct((HY_W, 2 * seq), F32),
        compiler_params=_params(("arbitrary",)),
        name="hyena_filters",
    )(*args)


def _hyena_kernel(cw_ref, cb_ref, x0_ref, x1_ref, v_ref, gate_ref, kext_ref, out_ref,
                  r_ref, t_ref, u2_ref, y2_ref, *, seq, batch):
    cblk = out_ref.shape[0]
    nb = seq // TOEP
    nq = 2 * seq // LANES
    half = seq // LANES
    lane = lax.broadcasted_iota(jnp.int32, (1, seq), 1)
    first = lane == 0
    last = lane == seq - 1
    row_i = lax.broadcasted_iota(jnp.int32, (LANES, LANES), 0)
    col_i = lax.broadcasted_iota(jnp.int32, (LANES, LANES), 1)
    upper = col_i >= row_i
    c0 = pl.program_id(0) * cblk

    def short_conv(ref, c, part):
        a = ref[c].astype(F32)
        ch = part * HY_W + c0 + c
        left = jnp.where(first, 0.0, pltpu.roll(a, 1, 1))
        right = jnp.where(last, 0.0, pltpu.roll(a, seq - 1, 1))
        return left * cw_ref[0, ch] + a * cw_ref[1, ch] + right * cw_ref[2, ch] + cb_ref[ch]

    def channel(c, carry):
        krow = kext_ref[pl.ds(c, 1), :]
        for q in range(nq):
            a = jnp.broadcast_to(krow[:, q * LANES:(q + 1) * LANES], (LANES, LANES))
            r_ref[q] = pltpu.roll(a, 0, 1, stride=1, stride_axis=0)
        for e in range(-(2 * nb - 1), 2 * nb):
            q0 = half + e
            g = jnp.where(upper, r_ref[q0], r_ref[q0 - 1]).astype(BF16)
            if e % 2 == 0:
                d = e // 2
                t_ref[d + nb - 1, 0:LANES, 0:LANES] = g
                t_ref[d + nb - 1, LANES:TOEP, LANES:TOEP] = g
            else:
                d = (e - 1) // 2
                if abs(d) <= nb - 1:
                    t_ref[d + nb - 1, 0:LANES, LANES:TOEP] = g
                d = (e + 1) // 2
                if abs(d) <= nb - 1:
                    t_ref[d + nb - 1, LANES:TOEP, 0:LANES] = g
        u = (short_conv(x1_ref, c, 1) * short_conv(v_ref, c, 2)).astype(BF16)
        for j in range(nb):
            u2_ref[j * batch:(j + 1) * batch, :] = u[:, j * TOEP:(j + 1) * TOEP]
        y2_ref[...] = _dot(u2_ref[...], t_ref[nb - 1])
        for d in list(range(1, nb)) + list(range(-(nb - 1), 0)):
            j_lo, j_hi = max(0, -d), nb - max(0, d)
            res = _dot(u2_ref[j_lo * batch:j_hi * batch, :], t_ref[d + nb - 1])
            y2_ref[(j_lo + d) * batch:(j_hi + d) * batch, :] += res
        x0 = short_conv(x0_ref, c, 0)
        gate = _silu(gate_ref[c].astype(F32))
        for j in range(nb):
            sl = slice(j * TOEP, (j + 1) * TOEP)
            out_ref[c, :, sl] = (x0[:, sl] * y2_ref[j * batch:(j + 1) * batch, :] * gate[:, sl]).astype(BF16)
        return carry

    lax.fori_loop(0, cblk, channel, 0)


def _hyena(zhyt, conv_w, conv_b, kext, batch, seq):
    cblk = 8
    nb = seq // TOEP
    z4 = zhyt.reshape(4, HY_W, batch, seq)
    part = lambda p: pl.BlockSpec((None, cblk, batch, seq), lambda i: (p, i, 0, 0))
    smem = pl.BlockSpec(memory_space=pltpu.SMEM)
    return pl.pallas_call(
        functools.partial(_hyena_kernel, seq=seq, batch=batch),
        grid=(HY_W // cblk,),
        in_specs=[smem, smem, part(0), part(1), part(2), part(3),
                  pl.BlockSpec((cblk, 2 * seq), lambda i: (i, 0))],
        out_specs=pl.BlockSpec((cblk, batch, seq), lambda i: (i, 0, 0)),
        out_shape=jax.ShapeDtypeStruct((HY_W, batch, seq), BF16),
        scratch_shapes=[
            pltpu.VMEM((2 * seq // LANES, LANES, LANES), F32),
            pltpu.VMEM((2 * nb - 1, TOEP, TOEP), BF16),
            pltpu.VMEM((nb * batch, TOEP), BF16),
            pltpu.VMEM((nb * batch, TOEP), F32),
        ],
        compiler_params=_params(("arbitrary",)),
        name="hyena",
    )(conv_w, conv_b, z4, z4, z4, z4, kext)


def _retention_tables(af_ref, ab_ref, tab_ref):
    k_scale = RET_HD ** -0.5
    ri = lax.broadcasted_iota(jnp.int32, (CHUNK, CHUNK), 0).astype(F32)
    ci = lax.broadcasted_iota(jnp.int32, (CHUNK, CHUNK), 1).astype(F32)
    diff = ri - ci
    for h in range(RET_H):
        log_f = -jnp.exp(jnp.full((CHUNK, CHUNK), af_ref[h], F32))
        log_b = -jnp.exp(jnp.full((CHUNK, CHUNK), ab_ref[h], F32))
        dec_f = jnp.where(diff >= 0, jnp.exp(log_f * jnp.maximum(diff, 0.0)), 0.0)
        dec_b = jnp.where(diff <= 0, jnp.exp(log_b * jnp.maximum(-diff, 0.0)), 0.0)
        tab_ref[h, T_DEC] = (dec_f + dec_b) * k_scale
        tab_ref[h, T_QDF] = jnp.exp(log_f * (ri + 1.0))
        tab_ref[h, T_QDB] = jnp.exp(log_b * (CHUNK - ri))
        tab_ref[h, T_KDF] = jnp.exp(log_f * (CHUNK - 1.0 - ri)) * k_scale
        tab_ref[h, T_KDB] = jnp.exp(log_b * ri) * k_scale
        tab_ref[h, T_CDF] = jnp.exp(log_f * CHUNK)
        tab_ref[h, T_CDB] = jnp.exp(log_b * CHUNK)


def _retention_kernel(af_ref, ab_ref, zc_ref, zx_ref, ng_ref, oc_ref, ox_ref,
                      tab_ref, sf_ref, sb_ref, sbc_ref, sbx_ref):
    hd = RET_HD

    @pl.when(pl.program_id(0) == 0)
    def _():
        _retention_tables(af_ref, ab_ref, tab_ref)

    sf_ref[...] = jnp.zeros_like(sf_ref)
    sb_ref[...] = jnp.zeros_like(sb_ref)

    def head_cols(part, h):
        return slice((part * RET_H + h) * hd, (part * RET_H + h + 1) * hd)

    def state_update(s, k, v, kd, cd):
        kdt = jnp.transpose(k.astype(F32) * kd).astype(BF16)
        return s * cd + _dot(kdt, v)

    def state_pass(z_ref, store_ref):
        nchunk = z_ref.shape[0] // CHUNK

        def body(i, carry):
            for t, s_ref, lo, t_kd, t_cd in ((i, sf_ref, 0, T_KDF, T_CDF),
                                             (nchunk - 1 - i, sb_ref, hd, T_KDB, T_CDB)):
                rows = pl.ds(pl.multiple_of(t * CHUNK, CHUNK), CHUNK)
                for h in range(RET_H):
                    k = z_ref[rows, head_cols(1, h)]
                    v = z_ref[rows, head_cols(2, h)]
                    s = s_ref[h]
                    store_ref[t, h, :, lo:lo + hd] = s.astype(BF16)
                    s_ref[h] = state_update(s, k, v, tab_ref[h, t_kd], tab_ref[h, t_cd])
            return carry

        lax.fori_loop(0, nchunk, body, 0, unroll=2)

    def out_pass(z_ref, store_ref, out_ref):
        nchunk = z_ref.shape[0] // CHUNK

        def body(t, carry):
            rows = pl.ds(pl.multiple_of(t * CHUNK, CHUNK), CHUNK)
            for h in range(RET_H):
                q = z_ref[rows, head_cols(0, h)]
                k = z_ref[rows, head_cols(1, h)]
                v = z_ref[rows, head_cols(2, h)]
                scores = _dot_nt(q, k) * tab_ref[h, T_DEC]
                cross = _dot(q, store_ref[t, h])
                o = (_dot(scores.astype(BF16), v) + cross[:, :hd] * tab_ref[h, T_QDF]
                     + cross[:, hd:] * tab_ref[h, T_QDB])
                mu = jnp.mean(o, axis=-1, keepdims=True)
                oc = o - mu
                var = jnp.mean(oc * oc, axis=-1, keepdims=True)
                y = oc * lax.rsqrt(var + EPS) * ng_ref[:, h * hd:(h + 1) * hd]
                gate = _silu(z_ref[rows, head_cols(3, h)].astype(F32))
                out_ref[rows, h * hd:(h + 1) * hd] = (y * gate).astype(BF16)
            return carry

        lax.fori_loop(0, nchunk, body, 0, unroll=2)

    state_pass(zc_ref, sbc_ref)
    state_pass(zx_ref, sbx_ref)
    out_pass(zc_ref, sbc_ref, oc_ref)
    out_pass(zx_ref, sbx_ref, ox_ref)


def _retention(zret_c, zret_x, a_f, a_b, norm_g, batch, ctx_len, seq):
    hd = RET_HD
    smem = pl.BlockSpec(memory_space=pltpu.SMEM)
    return pl.pallas_call(
        _retention_kernel,
        grid=(batch,),
        in_specs=[smem, smem,
                  pl.BlockSpec((ctx_len, 4 * RET_W), lambda b: (b, 0)),
                  pl.BlockSpec((seq, 4 * RET_W), lambda b: (b, 0)),
                  pl.BlockSpec((1, RET_W), lambda b: (0, 0))],
        out_specs=[pl.BlockSpec((ctx_len, RET_W), lambda b: (b, 0)),
                   pl.BlockSpec((seq, RET_W), lambda b: (b, 0))],
        out_shape=[jax.ShapeDtypeStruct((batch * ctx_len, RET_W), BF16),
                   jax.ShapeDtypeStruct((batch * seq, RET_W), BF16)],
        scratch_shapes=[
            pltpu.VMEM((RET_H, N_TABLES, CHUNK, CHUNK), F32),
            pltpu.VMEM((RET_H, hd, hd), F32),
            pltpu.VMEM((RET_H, hd, hd), F32),
            pltpu.VMEM((ctx_len // CHUNK, RET_H, hd, 2 * hd), BF16),
            pltpu.VMEM((seq // CHUNK, RET_H, hd, 2 * hd), BF16),
        ],
        compiler_params=_params(("arbitrary",)),
        name="retention",
    )(a_f, a_b, zret_c, zret_x, norm_g.reshape(1, RET_W))


def _head_norm_rope(z, gain, cos, sin):
    ms = jnp.mean(z * z, axis=-1, keepdims=True)
    y = z * lax.rsqrt(ms + EPS) * gain
    if cos is None:
        return y
    return y * cos + pltpu.roll(y, HEAD_DIM // 2, 1) * sin


def _store_v_aug(v_ref, v):
    ones = jnp.ones((v.shape[0], HEAD_DIM), BF16)
    for hd in range(N_KV):
        v_ref[:, 2 * hd * HEAD_DIM:(2 * hd + 1) * HEAD_DIM] = v[:, hd * HEAD_DIM:(hd + 1) * HEAD_DIM].astype(BF16)
        v_ref[:, (2 * hd + 1) * HEAD_DIM:(2 * hd + 2) * HEAD_DIM] = ones


def _mid_kernel(x_ref, yhyt_ref, yret_ref, mod0_ref, mod1_ref, wo_hy_ref, wo_ret_ref, g1_ref, win_ref,
                qn_ref, kn_ref, *rest, latent):
    d = x_ref.shape[-1]
    if latent:
        cos_ref, sin_ref, xo_ref, q_ref, k_ref, v_ref, g_ref = rest
    else:
        k_ref, v_ref = rest
    tm = x_ref.shape[0]
    sub = min(tm, MID_ROWS)
    for r in range(0, tm, sub):
        rs = slice(r, r + sub)
        cos, sin = (cos_ref[rs, :], sin_ref[rs, :]) if latent else (None, None)
        y = _dot_tn(yhyt_ref[:, rs], wo_hy_ref[...]) + _dot(yret_ref[rs, :], wo_ret_ref[...])
        x = x_ref[rs, :] + mod0_ref[:, 2 * d:3 * d] * y
        if latent:
            xo_ref[rs, :] = x
        h = _mod_norm(x, g1_ref[...], mod1_ref[:, 0:d], mod1_ref[:, d:2 * d]).astype(BF16)
        off = 0
        pair = 2 * HEAD_DIM
        if latent:
            for hp in range(N_HEADS // 2):
                z2 = _dot(h, win_ref[:, off:off + pair])
                for i in range(2):
                    q = _head_norm_rope(z2[:, i * HEAD_DIM:(i + 1) * HEAD_DIM], qn_ref[...], cos, sin) * Q_PRESCALE
                    q_ref[rs, off + i * HEAD_DIM:off + (i + 1) * HEAD_DIM] = q.astype(BF16)
                off += pair
        z2 = _dot(h, win_ref[:, off:off + pair])
        for i in range(N_KV):
            k = _head_norm_rope(z2[:, i * HEAD_DIM:(i + 1) * HEAD_DIM], kn_ref[...], cos, sin)
            k_ref[rs, i * HEAD_DIM:(i + 1) * HEAD_DIM] = k.astype(BF16)
        off += pair
        _store_v_aug(v_ref.at[rs, :], _dot(h, win_ref[:, off:off + KV_W]))
        off += KV_W
        if latent:
            g_ref[rs, :] = _dot(h, win_ref[:, off:off + ATT_W]).astype(BF16)


def _mid(x2, yhyt, yret, mod4, mod_row, wo_hy, wo_ret, g1, win, qn, kn, cosf, sinf, batch, seq, tm, latent):
    d = x2.shape[-1]
    nt = seq // tm
    tok = lambda w: pl.BlockSpec((tm, w), lambda b, t: (b * nt + t, 0))
    const = lambda a: pl.BlockSpec(a.shape, lambda b, t: (0,) * a.ndim)
    modspec = lambda layer: pl.BlockSpec((None, None, 1, 3 * d), lambda b, t: (layer, mod_row(b), 0, 0))
    in_specs = [tok(d), pl.BlockSpec((HY_W, tm), lambda b, t: (0, b * nt + t)), tok(RET_W),
                modspec(0), modspec(1), const(wo_hy), const(wo_ret), const(g1), const(win), const(qn), const(kn)]
    args = [x2, yhyt, yret, mod4, mod4, wo_hy, wo_ret, g1, win, qn, kn]
    n = batch * seq
    if latent:
        in_specs += [pl.BlockSpec((tm, HEAD_DIM), lambda b, t: (t, 0))] * 2
        args += [cosf, sinf]
        out_specs = [tok(d), tok(ATT_W), tok(KV_W), tok(2 * KV_W), tok(ATT_W)]
        out_shape = [jax.ShapeDtypeStruct((n, d), F32), jax.ShapeDtypeStruct((n, ATT_W), BF16),
                     jax.ShapeDtypeStruct((n, KV_W), BF16), jax.ShapeDtypeStruct((n, 2 * KV_W), BF16),
                     jax.ShapeDtypeStruct((n, ATT_W), BF16)]
    else:
        out_specs = [tok(KV_W), tok(2 * KV_W)]
        out_shape = [jax.ShapeDtypeStruct((n, KV_W), BF16), jax.ShapeDtypeStruct((n, 2 * KV_W), BF16)]
    return pl.pallas_call(
        functools.partial(_mid_kernel, latent=latent),
        grid=(batch, nt),
        in_specs=in_specs,
        out_specs=out_specs,
        out_shape=out_shape,
        compiler_params=_params(("arbitrary", "arbitrary")),
        name="mid_latent" if latent else "mid_ctx",
    )(*args)


def _attn_kernel(q_ref, kc_ref, vc_ref, kx_ref, vx_ref, o_ref):
    for hd in range(N_HEADS):
        kvh = hd // GROUP
        ksl = slice(kvh * HEAD_DIM, (kvh + 1) * HEAD_DIM)
        vsl = slice(2 * kvh * HEAD_DIM, 2 * (kvh + 1) * HEAD_DIM)
        sl = slice(hd * HEAD_DIM, (hd + 1) * HEAD_DIM)
        q = q_ref[:, sl]
        s_c = _dot_nt(q, kc_ref[:, ksl])
        s_x = _dot_nt(q, kx_ref[:, ksl])
        m = jnp.maximum(jnp.max(s_c, axis=-1, keepdims=True), jnp.max(s_x, axis=-1, keepdims=True))
        p_c = jnp.exp2(s_c - m).astype(BF16)
        p_x = jnp.exp2(s_x - m).astype(BF16)
        o = _dot(p_c, vc_ref[:, vsl]) + _dot(p_x, vx_ref[:, vsl])
        o_ref[:, sl] = (o[:, :HEAD_DIM] / o[:, HEAD_DIM:]).astype(BF16)


def _attention(q, kc, vc, kx, vx, batch, ctx_len, seq, tq):
    nt = seq // tq
    kv = lambda rows, w: pl.BlockSpec((rows, w), lambda b, t: (b, 0))
    return pl.pallas_call(
        _attn_kernel,
        grid=(batch, nt),
        in_specs=[pl.BlockSpec((tq, ATT_W), lambda b, t: (b * nt + t, 0)),
                  kv(ctx_len, KV_W), kv(ctx_len, 2 * KV_W), kv(seq, KV_W), kv(seq, 2 * KV_W)],
        out_specs=pl.BlockSpec((tq, ATT_W), lambda b, t: (b * nt + t, 0)),
        out_shape=jax.ShapeDtypeStruct((batch * seq, ATT_W), BF16),
        compiler_params=_params(("arbitrary", "arbitrary")),
        name="attention",
    )(q, kc, vc, kx, vx)


def _final_kernel(x_ref, o_ref, g_ref, mod_ref, w_ref, fg_ref, out_ref):
    d = x_ref.shape[-1]
    tm = x_ref.shape[0]
    sub = min(tm, MID_ROWS)
    for r in range(0, tm, sub):
        rs = slice(r, r + sub)
        a = (_silu(g_ref[rs, :].astype(F32)) * o_ref[rs, :].astype(F32)).astype(BF16)
        x = x_ref[rs, :] + mod_ref[:, 2 * d:3 * d] * _dot(a, w_ref[...])
        ms = jnp.mean(x * x, axis=-1, keepdims=True)
        out_ref[rs, :] = x * lax.rsqrt(ms + EPS) * fg_ref[...]


def _final(x2, o, g, mod4, w, fg, batch, seq, tm):
    d = x2.shape[-1]
    nt = seq // tm
    tok = lambda w_: pl.BlockSpec((tm, w_), lambda b, t: (b * nt + t, 0))
    const = lambda a: pl.BlockSpec(a.shape, lambda b, t: (0,) * a.ndim)
    return pl.pallas_call(
        _final_kernel,
        grid=(batch, nt),
        in_specs=[tok(d), tok(ATT_W), tok(ATT_W),
                  pl.BlockSpec((None, None, 1, 3 * d), lambda b, t: (1, b, 0, 0)), const(w), const(fg)],
        out_specs=tok(d),
        out_shape=jax.ShapeDtypeStruct((batch * seq, d), F32),
        compiler_params=_params(("arbitrary", "arbitrary")),
        name="final",
    )(x2, o, g, mod4, w, fg)


def _rope_tables(seq):
    rows = seq // GRID_W
    r = jnp.repeat(jnp.arange(rows, dtype=F32), GRID_W)
    col = jnp.tile(jnp.arange(GRID_W, dtype=F32), rows)
    half = HEAD_DIM // 2
    inv = ROPE_THETA ** (-jnp.arange(0, half, 2, dtype=F32) / half)
    ang = jnp.concatenate([r[:, None] * inv, col[:, None] * inv], axis=-1)
    cos, sin = jnp.cos(ang), jnp.sin(ang)
    return jnp.concatenate([cos, cos], axis=-1), jnp.concatenate([-sin, sin], axis=-1)


def kernel(x, c, ctx, c_ctx, norm_g, ada_w, ada_b, er_in_w, er_out_w, hy_conv_w, hy_conv_b, hy_f_w1, hy_f_b1, hy_f_freq, hy_f_w2, hy_f_b2, hy_f_w3, hy_bias, ret_decay_f, ret_decay_b, ret_norm_g, at_in_w, at_out_w, at_q_norm, at_k_norm, final_norm_g):
    batch, seq, d = x.shape
    ctx_len = ctx.shape[1]
    assert seq % TOEP == 0 and ctx_len % TOEP == 0 and seq % GRID_W == 0
    tm = min(512, seq)

    rows = -(-(batch + 1) // 8) * 8
    cond = jnp.concatenate([c, c_ctx[None, :], jnp.zeros((rows - batch - 1, d), F32)], axis=0)
    mod4 = _ada(cond, ada_w, ada_b).reshape(ada_w.shape[0], rows, 1, 3 * d)
    lat_row = lambda b: b
    ctx_row = lambda b: batch

    x2 = x.reshape(batch * seq, d)
    c2 = ctx.reshape(batch * ctx_len, d)

    in_w = er_in_w[0]
    whyt = in_w[:, :4 * HY_W].T.astype(BF16)
    wret = in_w[:, 4 * HY_W:].astype(BF16)
    g0 = norm_g[0].reshape(1, d)
    zhyt_x, zret_x = _inproj0(x2, mod4, 0, lat_row, g0, whyt, wret, batch, seq, tm)
    zhyt_c, zret_c = _inproj0(c2, mod4, 0, ctx_row, g0, whyt, wret, batch, ctx_len, ctx_len)

    filt = (hy_f_w1[0], hy_f_b1[0], hy_f_freq[0], hy_f_w2[0], hy_f_b2[0], hy_f_w3[0], hy_bias[0])
    yhyt_x = _hyena(zhyt_x, hy_conv_w[0], hy_conv_b[0], _filters(seq, *filt), batch, seq)
    yhyt_c = _hyena(zhyt_c, hy_conv_w[0], hy_conv_b[0], _filters(ctx_len, *filt), batch, ctx_len)
    yret_c, yret_x = _retention(zret_c, zret_x, ret_decay_f[0], ret_decay_b[0], ret_norm_g[0], batch, ctx_len, seq)

    wo_hy = er_out_w[0][:HY_W].astype(BF16)
    wo_ret = er_out_w[0][HY_W:].astype(BF16)
    perm = jnp.concatenate([jnp.arange(0, HEAD_DIM, 2), jnp.arange(1, HEAD_DIM, 2)])
    qk_cols = (jnp.arange(N_HEADS + N_KV)[:, None] * HEAD_DIM + perm[None, :]).reshape(-1)
    w1 = at_in_w[0]
    win_x = jnp.concatenate([w1[:, qk_cols], w1[:, ATT_W + KV_W:]], axis=1).astype(BF16)
    win_c = win_x[:, ATT_W:ATT_W + 2 * KV_W]
    qn = at_q_norm[0][perm].reshape(1, HEAD_DIM)
    kn = at_k_norm[0][perm].reshape(1, HEAD_DIM)
    g1 = norm_g[1].reshape(1, d)
    cosf, sinf = _rope_tables(seq)
    x1, q, kx, vx, gate = _mid(x2, yhyt_x.reshape(HY_W, batch * seq), yret_x, mod4, lat_row, wo_hy, wo_ret, g1,
                               win_x, qn, kn, cosf, sinf, batch, seq, tm, True)
    kc, vc = _mid(c2, yhyt_c.reshape(HY_W, batch * ctx_len), yret_c, mod4, ctx_row, wo_hy, wo_ret, g1,
                  win_c, qn, kn, None, None, batch, ctx_len, ctx_len, False)

    o = _attention(q, kc, vc, kx, vx, batch, ctx_len, seq, min(512, seq))
    out = _final(x1, o, gate, mod4, at_out_w[0].astype(BF16), final_norm_g.reshape(1, d), batch, seq, tm)
    return out.reshape(batch, seq, d)
```

```python
import functools
import math

import jax
import jax.numpy as jnp
from jax import lax
from jax.experimental import pallas as pl
from jax.experimental.pallas import tpu as pltpu

F32 = jnp.float32
BF16 = jnp.bfloat16
HIGHEST = lax.Precision.HIGHEST

EPS = 1e-6
GRID_W = 64
HY_W = 512
HY_EMB = 33
HY_BANDS = (HY_EMB - 1) // 2
HY_ORDER = 64
HY_DECAY_TARGET = 1e-2
HY_FAST_PCT = 0.3
HY_SLOW_PCT = 1.5
RET_W = 512
RET_H = 4
RET_HD = RET_W // RET_H
CHUNK = 128
N_HEADS = 8
N_KV = 2
HEAD_DIM = 128
GROUP = N_HEADS // N_KV
ATT_W = N_HEADS * HEAD_DIM
KV_W = N_KV * HEAD_DIM
ROPE_THETA = 10000.0
Q_PRESCALE = HEAD_DIM ** -0.5 * math.log2(math.e)

MID_ROWS = 256
HY_CH = 256
HY_PAR = 2
LANES = 128
TOEP = 256
VMEM_LIMIT = 56 * 1024 * 1024
T_DEC, T_QDF, T_QDB, T_KDF, T_KDB, T_CDF, T_CDB = range(7)
N_TABLES = 7


def _params(sem):
    return pltpu.CompilerParams(dimension_semantics=sem, vmem_limit_bytes=VMEM_LIMIT)


def _silu(x):
    return x * (1.0 / (1.0 + jnp.exp(-x)))


def _dot(a, b):
    return jnp.dot(a, b, preferred_element_type=F32)


def _dot_nt(a, b):
    return lax.dot_general(a, b, (((1,), (1,)), ((), ())), preferred_element_type=F32)


def _dot_tn(a, b):
    return lax.dot_general(a, b, (((0,), (0,)), ((), ())), preferred_element_type=F32)


def _mod_norm(x, g, shift, scale):
    ms = jnp.mean(x * x, axis=-1, keepdims=True)
    return (x * lax.rsqrt(ms + EPS) * g) * (1.0 + scale) + shift


def _ada_kernel(cond_ref, w_ref, b_ref, o_ref):
    s = _silu(cond_ref[...])
    o_ref[...] = jnp.dot(s, w_ref[...], preferred_element_type=F32, precision=HIGHEST) + b_ref[...]


def _ada(cond, ada_w, ada_b):
    depth, d, n = ada_w.shape
    rows = cond.shape[0]
    tn = 1024
    return pl.pallas_call(
        _ada_kernel,
        grid=(depth, n // tn),
        in_specs=[
            pl.BlockSpec((rows, d), lambda i, j: (0, 0)),
            pl.BlockSpec((None, d, tn), lambda i, j: (i, 0, j)),
            pl.BlockSpec((None, 1, tn), lambda i, j: (i, 0, j)),
        ],
        out_specs=pl.BlockSpec((None, rows, tn), lambda i, j: (i, 0, j)),
        out_shape=jax.ShapeDtypeStruct((depth, rows, n), F32),
        compiler_params=_params(("arbitrary", "arbitrary")),
        name="ada",
    )(cond, ada_w, ada_b.reshape(depth, 1, n))


def _inproj0_kernel(x_ref, mod_ref, g_ref, w_ref, cw_ref, cb_ref, hy_ref, zret_ref, h_ref):
    j = pl.program_id(1)
    seq, d = x_ref.shape
    rows = min(seq, 2 * MID_ROWS)

    @pl.when(j == 0)
    def _():
        for r in range(0, seq, rows):
            rs = slice(r, r + rows)
            h_ref[rs, :] = _mod_norm(x_ref[rs, :], g_ref[...], mod_ref[:, 0:d], mod_ref[:, d:2 * d]).astype(BF16)

    lane = lax.broadcasted_iota(jnp.int32, (1, seq), 1)
    first = lane == 0
    last = lane == seq - 1

    def conv(a, part, c0):
        ch = slice(part * HY_W + c0, part * HY_W + c0 + a.shape[0])
        left = jnp.where(first, 0.0, pltpu.roll(a, 1, 1))
        right = jnp.where(last, 0.0, pltpu.roll(a, seq - 1, 1))
        return left * cw_ref[ch, 0:1] + a * cw_ref[ch, 1:2] + right * cw_ref[ch, 2:3] + cb_ref[ch, :]

    def hyena_step(combine):
        h = h_ref[...]
        sub = 8
        for c0 in range(0, HY_W, HY_CH):
            za = _dot_nt(w_ref[c0:c0 + HY_CH, :], h)
            zb = _dot_nt(w_ref[HY_W + c0:HY_W + c0 + HY_CH, :], h)
            for r in range(0, HY_CH, 2 * sub):
                pieces = [combine(za[r + i:r + i + sub, :], zb[r + i:r + i + sub, :], c0 + r + i) for i in (0, sub)]
                hy_ref[c0 + r:c0 + r + 2 * sub, :] = jnp.concatenate(pieces, axis=0).astype(BF16)

    @pl.when(j == 0)
    def _():
        hyena_step(lambda x0, gate, c0: conv(x0, 0, c0) * _silu(gate))

    @pl.when(j == 1)
    def _():
        hyena_step(lambda x1, v, c0: conv(x1, 1, c0) * conv(v, 2, c0))

    @pl.when(j >= 2)
    def _():
        for r in range(0, seq, rows):
            rs = slice(r, r + rows)
            zret_ref[rs, :] = _dot(h_ref[rs, :], w_ref[...]).astype(BF16)


def _inproj0(x2, mod4, layer, mod_row, g, w4, cw, cb, batch, seq):
    d = x2.shape[-1]
    half = 2 * HY_W
    return pl.pallas_call(
        _inproj0_kernel,
        grid=(batch, 4),
        in_specs=[
            pl.BlockSpec((seq, d), lambda b, j: (b, 0), pipeline_mode=pl.Buffered(1)),
            pl.BlockSpec((None, None, 1, 3 * d), lambda b, j: (layer, mod_row(b), 0, 0)),
            pl.BlockSpec((1, d), lambda b, j: (0, 0)),
            pl.BlockSpec((None, half, d), lambda b, j: (j, 0, 0)),
            pl.BlockSpec(cw.shape, lambda b, j: (0, 0)),
            pl.BlockSpec(cb.shape, lambda b, j: (0, 0)),
        ],
        out_specs=[
            pl.BlockSpec((None, HY_W, seq), lambda b, j: (jnp.minimum(j, 1), 0, b)),
            pl.BlockSpec((seq, half), lambda b, j: (b, jnp.maximum(j - 2, 0))),
        ],
        out_shape=[
            jax.ShapeDtypeStruct((2, HY_W, batch * seq), BF16),
            jax.ShapeDtypeStruct((batch * seq, 2 * half), BF16),
        ],
        scratch_shapes=[pltpu.VMEM((seq, d), BF16)],
        compiler_params=_params(("arbitrary", "arbitrary")),
        name="inproj0",
    )(x2, mod4, g, w4, cw, cb)


def _filter_kernel(w1t_ref, w1c_ref, w1s_ref, b1_ref, f_ref, w2_ref, b2_ref, w3_ref, bias_ref, o_ref, *, seq):
    two_l = o_ref.shape[-1]
    cb = o_ref.shape[0]
    lag = lax.broadcasted_iota(jnp.int32, (1, two_l), 1) - seq
    n = jnp.abs(lag).astype(F32)
    t = n / float(max(seq - 1, 1))
    band_i = lax.broadcasted_iota(jnp.int32, (HY_BANDS, 1), 0).astype(F32)
    bands = 1e-4 + band_i * ((HY_BANDS - 1 - 1e-4) / (HY_BANDS - 1))
    ang = (2.0 * math.pi / seq) * n * bands
    f = f_ref[...]
    pre = (w1t_ref[...] * t
           + jnp.dot(w1c_ref[...], jnp.cos(ang), preferred_element_type=F32, precision=HIGHEST)
           + jnp.dot(w1s_ref[...], -jnp.sin(ang), preferred_element_type=F32, precision=HIGHEST))
    h = jnp.sin(f * (pre + b1_ref[...]))
    h = jnp.sin(f * (jnp.dot(w2_ref[...], h, preferred_element_type=F32, precision=HIGHEST) + b2_ref[...]))
    h_f = jnp.dot(w3_ref[0], h, preferred_element_type=F32, precision=HIGHEST)
    h_b = jnp.dot(w3_ref[1], h, preferred_element_type=F32, precision=HIGHEST)
    max_decay = math.log(HY_DECAY_TARGET) / HY_FAST_PCT
    min_decay = math.log(HY_DECAY_TARGET) / HY_SLOW_PCT
    ch = (lax.broadcasted_iota(jnp.int32, (cb, 1), 0) + pl.program_id(0) * cb).astype(F32)
    deltas = jnp.abs(min_decay + ch * ((max_decay - min_decay) / (HY_W - 1)))
    window = jnp.exp(-t * deltas)
    k = jnp.where(lag >= 0, h_f, h_b) * window
    k = jnp.where(lag == -seq, 0.0, k)
    o_ref[...] = k + jnp.where(lag == 0, bias_ref[...], 0.0)


def _filters(seq, w1, b1, freq, w2, b2, w3, bias):
    cb = 128
    w1t = w1.T
    args = (
        w1t[:, 0:1], w1t[:, 1:1 + HY_BANDS], w1t[:, 1 + HY_BANDS:],
        b1.reshape(HY_ORDER, 1), freq.reshape(HY_ORDER, 1), w2.T, b2.reshape(HY_ORDER, 1),
        w3.T.reshape(2, HY_W, HY_ORDER), bias.reshape(HY_W, 1),
    )
    full = lambda a: pl.BlockSpec(a.shape, lambda i: (0,) * a.ndim)
    in_specs = [full(a) for a in args[:7]] + [
        pl.BlockSpec((2, cb, HY_ORDER), lambda i: (0, i, 0)),
        pl.BlockSpec((cb, 1), lambda i: (i, 0)),
    ]
    return pl.pallas_call(
        functools.partial(_filter_kernel, seq=seq),
        grid=(HY_W // cb,),
        in_specs=in_specs,
        out_specs=pl.BlockSpec((cb, 2 * seq), lambda i: (i, 0)),
        out_shape=jax.ShapeDtypeStruct((HY_W, 2 * seq), F32),
        compiler_params=_params(("arbitrary",)),
        name="hyena_filters",
    )(*args)


def _hyena_kernel(xg_ref, u_ref, kext_ref, out_ref, t_all, u2_all, y2_all, *, seq, batch):
    cblk = out_ref.shape[0]
    nb = seq // TOEP
    half = seq // LANES
    row_i = lax.broadcasted_iota(jnp.int32, (LANES, LANES), 0)
    col_i = lax.broadcasted_iota(jnp.int32, (LANES, LANES), 1)
    upper = col_i >= row_i

    def channel(c, t_ref, u2_ref, y2_ref):
        krow = kext_ref[pl.ds(c, 1), :]

        def rotated(q):
            a = jnp.broadcast_to(krow[:, q * LANES:(q + 1) * LANES], (LANES, LANES))
            return pltpu.roll(a, 0, 1, stride=1, stride_axis=0)

        e_lo = -(2 * nb - 1)
        r_prev = rotated(half + e_lo - 1)
        for e in range(e_lo, 2 * nb):
            r_cur = rotated(half + e)
            g = jnp.where(upper, r_cur, r_prev).astype(BF16)
            r_prev = r_cur
            if e % 2 == 0:
                d = e // 2
                t_ref[d + nb - 1, 0:LANES, 0:LANES] = g
                t_ref[d + nb - 1, LANES:TOEP, LANES:TOEP] = g
            else:
                d = (e - 1) // 2
                if abs(d) <= nb - 1:
                    t_ref[d + nb - 1, 0:LANES, LANES:TOEP] = g
                d = (e + 1) // 2
                if abs(d) <= nb - 1:
                    t_ref[d + nb - 1, LANES:TOEP, 0:LANES] = g
        for j in range(nb):
            u2_ref[j * batch:(j + 1) * batch, :] = u_ref[c, :, j * TOEP:(j + 1) * TOEP]
        y2_ref[...] = _dot(u2_ref[...], t_ref[nb - 1])
        for d in list(range(1, nb)) + list(range(-(nb - 1), 0)):
            j_lo, j_hi = max(0, -d), nb - max(0, d)
            res = _dot(u2_ref[j_lo * batch:j_hi * batch, :], t_ref[d + nb - 1])
            y2_ref[(j_lo + d) * batch:(j_hi + d) * batch, :] += res
        for j in range(nb):
            sl = slice(j * TOEP, (j + 1) * TOEP)
            out_ref[c, :, sl] = (xg_ref[c, :, sl].astype(F32) * y2_ref[j * batch:(j + 1) * batch, :]).astype(BF16)

    def group(i, carry):
        for k in range(HY_PAR):
            channel(i * HY_PAR + k, t_all.at[k], u2_all.at[k], y2_all.at[k])
        return carry

    lax.fori_loop(0, cblk // HY_PAR, group, 0)


def _hyena(hy, kext, batch, seq):
    cblk = 8
    nb = seq // TOEP
    hy4 = hy.reshape(2, HY_W, batch, seq)
    part = lambda p: pl.BlockSpec((None, cblk, batch, seq), lambda i: (p, i, 0, 0))
    return pl.pallas_call(
        functools.partial(_hyena_kernel, seq=seq, batch=batch),
        grid=(HY_W // cblk,),
        in_specs=[part(0), part(1), pl.BlockSpec((cblk, 2 * seq), lambda i: (i, 0))],
        out_specs=pl.BlockSpec((cblk, batch, seq), lambda i: (i, 0, 0)),
        out_shape=jax.ShapeDtypeStruct((HY_W, batch, seq), BF16),
        scratch_shapes=[
            pltpu.VMEM((HY_PAR, 2 * nb - 1, TOEP, TOEP), BF16),
            pltpu.VMEM((HY_PAR, nb * batch, TOEP), BF16),
            pltpu.VMEM((HY_PAR, nb * batch, TOEP), F32),
        ],
        compiler_params=_params(("arbitrary",)),
        name="hyena",
    )(hy4, hy4, kext)


def _retention_tables(af_ref, ab_ref, tab_ref):
    k_scale = RET_HD ** -0.5
    ri = lax.broadcasted_iota(jnp.int32, (CHUNK, CHUNK), 0).astype(F32)
    ci = lax.broadcasted_iota(jnp.int32, (CHUNK, CHUNK), 1).astype(F32)
    diff = ri - ci
    for h in range(RET_H):
        log_f = -jnp.exp(jnp.full((CHUNK, CHUNK), af_ref[h], F32))
        log_b = -jnp.exp(jnp.full((CHUNK, CHUNK), ab_ref[h], F32))
        dec_f = jnp.where(diff >= 0, jnp.exp(log_f * jnp.maximum(diff, 0.0)), 0.0)
        dec_b = jnp.where(diff <= 0, jnp.exp(log_b * jnp.maximum(-diff, 0.0)), 0.0)
        tab_ref[h, T_DEC] = (dec_f + dec_b) * k_scale
        tab_ref[h, T_QDF] = jnp.exp(log_f * (ri + 1.0))
        tab_ref[h, T_QDB] = jnp.exp(log_b * (CHUNK - ri))
        tab_ref[h, T_KDF] = jnp.exp(log_f * (CHUNK - 1.0 - ri)) * k_scale
        tab_ref[h, T_KDB] = jnp.exp(log_b * ri) * k_scale
        tab_ref[h, T_CDF] = jnp.exp(log_f * CHUNK)
        tab_ref[h, T_CDB] = jnp.exp(log_b * CHUNK)


def _retention_kernel(af_ref, ab_ref, zc_ref, zx_ref, ng_ref, oc_ref, ox_ref,
                      tab_ref, sf_ref, sb_ref, sbc_ref, sbx_ref):
    hd = RET_HD

    @pl.when(pl.program_id(0) == 0)
    def _():
        _retention_tables(af_ref, ab_ref, tab_ref)

    sf_ref[...] = jnp.zeros_like(sf_ref)
    sb_ref[...] = jnp.zeros_like(sb_ref)

    def head_cols(part, h):
        return slice((part * RET_H + h) * hd, (part * RET_H + h + 1) * hd)

    def state_update(s, k, v, kd, cd):
        kdt = jnp.transpose(k.astype(F32) * kd).astype(BF16)
        return s * cd + _dot(kdt, v)

    def state_pass(z_ref, store_ref):
        nchunk = z_ref.shape[0] // CHUNK

        def body(i, carry):
            for t, s_ref, lo, t_kd, t_cd in ((i, sf_ref, 0, T_KDF, T_CDF),
                                             (nchunk - 1 - i, sb_ref, hd, T_KDB, T_CDB)):
                rows = pl.ds(pl.multiple_of(t * CHUNK, CHUNK), CHUNK)
                for h in range(RET_H):
                    k = z_ref[rows, head_cols(1, h)]
                    v = z_ref[rows, head_cols(2, h)]
                    s = s_ref[h]
                    store_ref[t, h, :, lo:lo + hd] = s.astype(BF16)
                    s_ref[h] = state_update(s, k, v, tab_ref[h, t_kd], tab_ref[h, t_cd])
            return carry

        lax.fori_loop(0, nchunk, body, 0, unroll=2)

    def out_pass(z_ref, store_ref, out_ref):
        nchunk = z_ref.shape[0] // CHUNK

        def body(t, carry):
            rows = pl.ds(pl.multiple_of(t * CHUNK, CHUNK), CHUNK)
            for h in range(RET_H):
                q = z_ref[rows, head_cols(0, h)]
                k = z_ref[rows, head_cols(1, h)]
                v = z_ref[rows, head_cols(2, h)]
                scores = _dot_nt(q, k) * tab_ref[h, T_DEC]
                cross = _dot(q, store_ref[t, h])
                o = (_dot(scores.astype(BF16), v) + cross[:, :hd] * tab_ref[h, T_QDF]
                     + cross[:, hd:] * tab_ref[h, T_QDB])
                mu = jnp.mean(o, axis=-1, keepdims=True)
                oc = o - mu
                var = jnp.mean(oc * oc, axis=-1, keepdims=True)
                y = oc * lax.rsqrt(var + EPS) * ng_ref[:, h * hd:(h + 1) * hd]
                gate = _silu(z_ref[rows, head_cols(3, h)].astype(F32))
                out_ref[rows, h * hd:(h + 1) * hd] = (y * gate).astype(BF16)
            return carry

        lax.fori_loop(0, nchunk, body, 0, unroll=2)

    state_pass(zc_ref, sbc_ref)
    state_pass(zx_ref, sbx_ref)
    out_pass(zc_ref, sbc_ref, oc_ref)
    out_pass(zx_ref, sbx_ref, ox_ref)


def _retention(zret_c, zret_x, a_f, a_b, norm_g, batch, ctx_len, seq):
    hd = RET_HD
    smem = pl.BlockSpec(memory_space=pltpu.SMEM)
    return pl.pallas_call(
        _retention_kernel,
        grid=(batch,),
        in_specs=[smem, smem,
                  pl.BlockSpec((ctx_len, 4 * RET_W), lambda b: (b, 0)),
                  pl.BlockSpec((seq, 4 * RET_W), lambda b: (b, 0)),
                  pl.BlockSpec((1, RET_W), lambda b: (0, 0))],
        out_specs=[pl.BlockSpec((ctx_len, RET_W), lambda b: (b, 0)),
                   pl.BlockSpec((seq, RET_W), lambda b: (b, 0))],
        out_shape=[jax.ShapeDtypeStruct((batch * ctx_len, RET_W), BF16),
                   jax.ShapeDtypeStruct((batch * seq, RET_W), BF16)],
        scratch_shapes=[
            pltpu.VMEM((RET_H, N_TABLES, CHUNK, CHUNK), F32),
            pltpu.VMEM((RET_H, hd, hd), F32),
            pltpu.VMEM((RET_H, hd, hd), F32),
            pltpu.VMEM((ctx_len // CHUNK, RET_H, hd, 2 * hd), BF16),
            pltpu.VMEM((seq // CHUNK, RET_H, hd, 2 * hd), BF16),
        ],
        compiler_params=_params(("arbitrary",)),
        name="retention",
    )(a_f, a_b, zret_c, zret_x, norm_g.reshape(1, RET_W))


def _head_norm_rope(z, gain, cos, sin):
    ms = jnp.mean(z * z, axis=-1, keepdims=True)
    y = z * lax.rsqrt(ms + EPS) * gain
    if cos is None:
        return y
    return y * cos + pltpu.roll(y, HEAD_DIM // 2, 1) * sin


def _store_v_aug(v_ref, v):
    ones = jnp.ones((v.shape[0], HEAD_DIM), BF16)
    for hd in range(N_KV):
        v_ref[:, 2 * hd * HEAD_DIM:(2 * hd + 1) * HEAD_DIM] = v[:, hd * HEAD_DIM:(hd + 1) * HEAD_DIM].astype(BF16)
        v_ref[:, (2 * hd + 1) * HEAD_DIM:(2 * hd + 2) * HEAD_DIM] = ones


def _mid_kernel(x_ref, yhyt_ref, yret_ref, mod0_ref, mod1_ref, wo_hy_ref, wo_ret_ref, g1_ref, win_ref,
                qn_ref, kn_ref, *rest, latent):
    d = x_ref.shape[-1]
    if latent:
        cos_ref, sin_ref, xo_ref, q_ref, k_ref, v_ref, g_ref = rest
    else:
        k_ref, v_ref = rest
    tm = x_ref.shape[0]
    sub = min(tm, MID_ROWS)
    for r in range(0, tm, sub):
        rs = slice(r, r + sub)
        cos, sin = (cos_ref[rs, :], sin_ref[rs, :]) if latent else (None, None)
        y = _dot_tn(yhyt_ref[:, rs], wo_hy_ref[...]) + _dot(yret_ref[rs, :], wo_ret_ref[...])
        x = x_ref[rs, :] + mod0_ref[:, 2 * d:3 * d] * y
        if latent:
            xo_ref[rs, :] = x
        h = _mod_norm(x, g1_ref[...], mod1_ref[:, 0:d], mod1_ref[:, d:2 * d]).astype(BF16)
        off = 0
        pair = 2 * HEAD_DIM
        if latent:
            for hp in range(N_HEADS // 2):
                z2 = _dot(h, win_ref[:, off:off + pair])
                for i in range(2):
                    q = _head_norm_rope(z2[:, i * HEAD_DIM:(i + 1) * HEAD_DIM], qn_ref[...], cos, sin) * Q_PRESCALE
                    q_ref[rs, off + i * HEAD_DIM:off + (i + 1) * HEAD_DIM] = q.astype(BF16)
                off += pair
        z2 = _dot(h, win_ref[:, off:off + pair])
        for i in range(N_KV):
            k = _head_norm_rope(z2[:, i * HEAD_DIM:(i + 1) * HEAD_DIM], kn_ref[...], cos, sin)
            k_ref[rs, i * HEAD_DIM:(i + 1) * HEAD_DIM] = k.astype(BF16)
        off += pair
        _store_v_aug(v_ref.at[rs, :], _dot(h, win_ref[:, off:off + KV_W]))
        off += KV_W
        if latent:
            g_ref[rs, :] = _dot(h, win_ref[:, off:off + ATT_W]).astype(BF16)


def _mid(x2, yhyt, yret, mod4, mod_row, wo_hy, wo_ret, g1, win, qn, kn, cosf, sinf, batch, seq, tm, latent):
    d = x2.shape[-1]
    nt = seq // tm
    tok = lambda w: pl.BlockSpec((tm, w), lambda b, t: (b * nt + t, 0))
    const = lambda a: pl.BlockSpec(a.shape, lambda b, t: (0,) * a.ndim)
    modspec = lambda layer: pl.BlockSpec((None, None, 1, 3 * d), lambda b, t: (layer, mod_row(b), 0, 0))
    in_specs = [tok(d), pl.BlockSpec((HY_W, tm), lambda b, t: (0, b * nt + t)), tok(RET_W),
                modspec(0), modspec(1), const(wo_hy), const(wo_ret), const(g1), const(win), const(qn), const(kn)]
    args = [x2, yhyt, yret, mod4, mod4, wo_hy, wo_ret, g1, win, qn, kn]
    n = batch * seq
    if latent:
        in_specs += [pl.BlockSpec((tm, HEAD_DIM), lambda b, t: (t, 0))] * 2
        args += [cosf, sinf]
        out_specs = [tok(d), tok(ATT_W), tok(KV_W), tok(2 * KV_W), tok(ATT_W)]
        out_shape = [jax.ShapeDtypeStruct((n, d), F32), jax.ShapeDtypeStruct((n, ATT_W), BF16),
                     jax.ShapeDtypeStruct((n, KV_W), BF16), jax.ShapeDtypeStruct((n, 2 * KV_W), BF16),
                     jax.ShapeDtypeStruct((n, ATT_W), BF16)]
    else:
        out_specs = [tok(KV_W), tok(2 * KV_W)]
        out_shape = [jax.ShapeDtypeStruct((n, KV_W), BF16), jax.ShapeDtypeStruct((n, 2 * KV_W), BF16)]
    return pl.pallas_call(
        functools.partial(_mid_kernel, latent=latent),
        grid=(batch, nt),
        in_specs=in_specs,
        out_specs=out_specs,
        out_shape=out_shape,
        compiler_params=_params(("arbitrary", "arbitrary")),
        name="mid_latent" if latent else "mid_ctx",
    )(*args)


def _attn_kernel(q_ref, kc_ref, vc_ref, kx_ref, vx_ref, g_ref, x_ref, mod_ref, w_ref, fg_ref, out_ref):
    d = x_ref.shape[-1]
    heads = []
    for hd in range(N_HEADS):
        kvh = hd // GROUP
        ksl = slice(kvh * HEAD_DIM, (kvh + 1) * HEAD_DIM)
        vsl = slice(2 * kvh * HEAD_DIM, 2 * (kvh + 1) * HEAD_DIM)
        sl = slice(hd * HEAD_DIM, (hd + 1) * HEAD_DIM)
        q = q_ref[:, sl]
        s_c = _dot_nt(q, kc_ref[:, ksl])
        s_x = _dot_nt(q, kx_ref[:, ksl])
        m = jnp.maximum(jnp.max(s_c, axis=-1, keepdims=True), jnp.max(s_x, axis=-1, keepdims=True))
        p_c = jnp.exp2(s_c - m).astype(BF16)
        p_x = jnp.exp2(s_x - m).astype(BF16)
        o = _dot(p_c, vc_ref[:, vsl]) + _dot(p_x, vx_ref[:, vsl])
        gate = _silu(g_ref[:, sl].astype(F32))
        heads.append((gate * (o[:, :HEAD_DIM] / o[:, HEAD_DIM:])).astype(BF16))
    y = _dot(jnp.concatenate(heads, axis=1), w_ref[...])
    x = x_ref[...] + mod_ref[:, 2 * d:3 * d] * y
    ms = jnp.mean(x * x, axis=-1, keepdims=True)
    out_ref[...] = x * lax.rsqrt(ms + EPS) * fg_ref[...]


def _attention(q, kc, vc, kx, vx, g, x2, mod4, w, fg, batch, ctx_len, seq, tq):
    d = x2.shape[-1]
    nt = seq // tq
    kv = lambda rows, w_: pl.BlockSpec((rows, w_), lambda b, t: (b, 0))
    tok = lambda w_: pl.BlockSpec((tq, w_), lambda b, t: (b * nt + t, 0))
    const = lambda a: pl.BlockSpec(a.shape, lambda b, t: (0,) * a.ndim)
    return pl.pallas_call(
        _attn_kernel,
        grid=(batch, nt),
        in_specs=[tok(ATT_W), kv(ctx_len, KV_W), kv(ctx_len, 2 * KV_W), kv(seq, KV_W), kv(seq, 2 * KV_W),
                  tok(ATT_W), tok(d),
                  pl.BlockSpec((None, None, 1, 3 * d), lambda b, t: (1, b, 0, 0)), const(w), const(fg)],
        out_specs=tok(d),
        out_shape=jax.ShapeDtypeStruct((batch * seq, d), F32),
        compiler_params=_params(("arbitrary", "arbitrary")),
        name="attention",
    )(q, kc, vc, kx, vx, g, x2, mod4, w, fg)


def _rope_tables(seq):
    rows = seq // GRID_W
    r = jnp.repeat(jnp.arange(rows, dtype=F32), GRID_W)
    col = jnp.tile(jnp.arange(GRID_W, dtype=F32), rows)
    half = HEAD_DIM // 2
    inv = ROPE_THETA ** (-jnp.arange(0, half, 2, dtype=F32) / half)
    ang = jnp.concatenate([r[:, None] * inv, col[:, None] * inv], axis=-1)
    cos, sin = jnp.cos(ang), jnp.sin(ang)
    return jnp.concatenate([cos, cos], axis=-1), jnp.concatenate([-sin, sin], axis=-1)


def kernel(x, c, ctx, c_ctx, norm_g, ada_w, ada_b, er_in_w, er_out_w, hy_conv_w, hy_conv_b, hy_f_w1, hy_f_b1, hy_f_freq, hy_f_w2, hy_f_b2, hy_f_w3, hy_bias, ret_decay_f, ret_decay_b, ret_norm_g, at_in_w, at_out_w, at_q_norm, at_k_norm, final_norm_g):
    batch, seq, d = x.shape
    ctx_len = ctx.shape[1]
    assert seq % TOEP == 0 and ctx_len % TOEP == 0 and seq % GRID_W == 0 and d == 2 * HY_W
    tm = min(512, seq)

    rows = -(-(batch + 1) // 8) * 8
    cond = jnp.concatenate([c, c_ctx[None, :], jnp.zeros((rows - batch - 1, d), F32)], axis=0)
    mod4 = _ada(cond, ada_w, ada_b).reshape(ada_w.shape[0], rows, 1, 3 * d)
    lat_row = lambda b: b
    ctx_row = lambda b: batch

    x2 = x.reshape(batch * seq, d)
    c2 = ctx.reshape(batch * ctx_len, d)

    in_w = er_in_w[0].astype(BF16)
    col = lambda p: in_w[:, p * HY_W:(p + 1) * HY_W].T
    w4 = jnp.stack([jnp.concatenate([col(0), col(3)], axis=0), jnp.concatenate([col(1), col(2)], axis=0),
                    in_w[:, 4 * HY_W:4 * HY_W + d], in_w[:, 4 * HY_W + d:]])
    cw = hy_conv_w[0].T
    cb = hy_conv_b[0].reshape(3 * HY_W, 1)
    g0 = norm_g[0].reshape(1, d)
    hy_x, zret_x = _inproj0(x2, mod4, 0, lat_row, g0, w4, cw, cb, batch, seq)
    hy_c, zret_c = _inproj0(c2, mod4, 0, ctx_row, g0, w4, cw, cb, batch, ctx_len)

    filt = (hy_f_w1[0], hy_f_b1[0], hy_f_freq[0], hy_f_w2[0], hy_f_b2[0], hy_f_w3[0], hy_bias[0])
    yhyt_x = _hyena(hy_x, _filters(seq, *filt), batch, seq)
    yhyt_c = _hyena(hy_c, _filters(ctx_len, *filt), batch, ctx_len)
    yret_c, yret_x = _retention(zret_c, zret_x, ret_decay_f[0], ret_decay_b[0], ret_norm_g[0], batch, ctx_len, seq)

    wo_hy = er_out_w[0][:HY_W].astype(BF16)
    wo_ret = er_out_w[0][HY_W:].astype(BF16)
    perm = jnp.concatenate([jnp.arange(0, HEAD_DIM, 2), jnp.arange(1, HEAD_DIM, 2)])
    qk_cols = (jnp.arange(N_HEADS + N_KV)[:, None] * HEAD_DIM + perm[None, :]).reshape(-1)
    w1 = at_in_w[0]
    win_x = jnp.concatenate([w1[:, qk_cols], w1[:, ATT_W + KV_W:]], axis=1).astype(BF16)
    win_c = win_x[:, ATT_W:ATT_W + 2 * KV_W]
    qn = at_q_norm[0][perm].reshape(1, HEAD_DIM)
    kn = at_k_norm[0][perm].reshape(1, HEAD_DIM)
    g1 = norm_g[1].reshape(1, d)
    cosf, sinf = _rope_tables(seq)
    x1, q, kx, vx, gate = _mid(x2, yhyt_x.reshape(HY_W, batch * seq), yret_x, mod4, lat_row, wo_hy, wo_ret, g1,
                               win_x, qn, kn, cosf, sinf, batch, seq, tm, True)
    kc, vc = _mid(c2, yhyt_c.reshape(HY_W, batch * ctx_len), yret_c, mod4, ctx_row, wo_hy, wo_ret, g1,
                  win_c, qn, kn, None, None, batch, ctx_len, ctx_len, False)

    out = _attention(q, kc, vc, kx, vx, gate, x1, mod4, at_out_w[0].astype(BF16), final_norm_g.reshape(1, d),
                     batch, ctx_len, seq, min(512, seq))
    return out.reshape(batch, seq, d)
```

```python
import functools
import math

import jax
import jax.numpy as jnp
from jax import lax
from jax.experimental import pallas as pl
from jax.experimental.pallas import tpu as pltpu

F32 = jnp.float32
BF16 = jnp.bfloat16
HIGHEST = lax.Precision.HIGHEST

EPS = 1e-6
GRID_W = 64
HY_W = 512
HY_EMB = 33
HY_BANDS = (HY_EMB - 1) // 2
HY_ORDER = 64
HY_DECAY_TARGET = 1e-2
HY_FAST_PCT = 0.3
HY_SLOW_PCT = 1.5
RET_W = 512
RET_H = 4
RET_HD = RET_W // RET_H
CHUNK = 128
N_HEADS = 8
N_KV = 2
HEAD_DIM = 128
GROUP = N_HEADS // N_KV
ATT_W = N_HEADS * HEAD_DIM
KV_W = N_KV * HEAD_DIM
ROPE_THETA = 10000.0
Q_PRESCALE = HEAD_DIM ** -0.5 * math.log2(math.e)

MID_ROWS = 256
HY_CH = 512
HY_PAR = 2
LANES = 128
TOEP = 256
VMEM_LIMIT = 56 * 1024 * 1024
T_DEC, T_QDF, T_QDB, T_KDF, T_KDB, T_CDF, T_CDB = range(7)
N_TABLES = 7


def _params(sem):
    return pltpu.CompilerParams(dimension_semantics=sem, vmem_limit_bytes=VMEM_LIMIT)


def _silu(x):
    return x * (1.0 / (1.0 + jnp.exp(-x)))


def _dot(a, b):
    return jnp.dot(a, b, preferred_element_type=F32)


def _dot_nt(a, b):
    return lax.dot_general(a, b, (((1,), (1,)), ((), ())), preferred_element_type=F32)


def _dot_tn(a, b):
    return lax.dot_general(a, b, (((0,), (0,)), ((), ())), preferred_element_type=F32)


def _mod_norm(x, g, shift, scale):
    ms = jnp.mean(x * x, axis=-1, keepdims=True)
    return (x * lax.rsqrt(ms + EPS) * g) * (1.0 + scale) + shift


def _ada_kernel(cond_ref, w_ref, b_ref, o_ref):
    s = _silu(cond_ref[...])
    o_ref[...] = jnp.dot(s, w_ref[...], preferred_element_type=F32, precision=HIGHEST) + b_ref[...]


def _ada(cond, ada_w, ada_b):
    depth, d, n = ada_w.shape
    rows = cond.shape[0]
    tn = 1024
    return pl.pallas_call(
        _ada_kernel,
        grid=(depth, n // tn),
        in_specs=[
            pl.BlockSpec((rows, d), lambda i, j: (0, 0)),
            pl.BlockSpec((None, d, tn), lambda i, j: (i, 0, j)),
            pl.BlockSpec((None, 1, tn), lambda i, j: (i, 0, j)),
        ],
        out_specs=pl.BlockSpec((None, rows, tn), lambda i, j: (i, 0, j)),
        out_shape=jax.ShapeDtypeStruct((depth, rows, n), F32),
        compiler_params=_params(("arbitrary", "arbitrary")),
        name="ada",
    )(cond, ada_w, ada_b.reshape(depth, 1, n))


def _inproj0_kernel(x_ref, mod_ref, g_ref, w_ref, cw_ref, cb_ref, hy_ref, zret_ref, h_ref, *, period):
    j = pl.program_id(1)
    seq, d = x_ref.shape
    rows = min(seq, 2 * MID_ROWS)

    @pl.when(j == 0)
    def _():
        for r in range(0, seq, rows):
            rs = slice(r, r + rows)
            h_ref[rs, :] = _mod_norm(x_ref[rs, :], g_ref[...], mod_ref[:, 0:d], mod_ref[:, d:2 * d]).astype(BF16)

    lane = lax.broadcasted_iota(jnp.int32, (1, seq), 1)
    first = lane % period == 0
    last = lane % period == period - 1

    def conv(a, part, c0):
        ch = slice(part * HY_W + c0, part * HY_W + c0 + a.shape[0])
        left = jnp.where(first, 0.0, pltpu.roll(a, 1, 1))
        right = jnp.where(last, 0.0, pltpu.roll(a, seq - 1, 1))
        return left * cw_ref[ch, 0:1] + a * cw_ref[ch, 1:2] + right * cw_ref[ch, 2:3] + cb_ref[ch, :]

    def hyena_step(combine):
        h = h_ref[...]
        sub = 8
        for c0 in range(0, HY_W, HY_CH):
            za = _dot_nt(w_ref[c0:c0 + HY_CH, :], h)
            zb = _dot_nt(w_ref[HY_W + c0:HY_W + c0 + HY_CH, :], h)
            for r in range(0, HY_CH, 2 * sub):
                pieces = [combine(za[r + i:r + i + sub, :], zb[r + i:r + i + sub, :], c0 + r + i) for i in (0, sub)]
                hy_ref[c0 + r:c0 + r + 2 * sub, :] = jnp.concatenate(pieces, axis=0).astype(BF16)

    @pl.when(j == 0)
    def _():
        hyena_step(lambda x0, gate, c0: conv(x0, 0, c0) * _silu(gate))

    @pl.when(j == 1)
    def _():
        hyena_step(lambda x1, v, c0: conv(x1, 1, c0) * conv(v, 2, c0))

    @pl.when(j >= 2)
    def _():
        for r in range(0, seq, rows):
            rs = slice(r, r + rows)
            zret_ref[rs, :] = _dot(h_ref[rs, :], w_ref[...]).astype(BF16)


def _inproj0(x2, mod4, layer, mod_row, g, w4, cw, cb, batch, seq, period):
    d = x2.shape[-1]
    half = 2 * HY_W
    return pl.pallas_call(
        functools.partial(_inproj0_kernel, period=period),
        grid=(batch, 4),
        in_specs=[
            pl.BlockSpec((seq, d), lambda b, j: (b, 0), pipeline_mode=pl.Buffered(1)),
            pl.BlockSpec((None, None, 1, 3 * d), lambda b, j: (layer, mod_row(b), 0, 0)),
            pl.BlockSpec((1, d), lambda b, j: (0, 0)),
            pl.BlockSpec((None, half, d), lambda b, j: (j, 0, 0)),
            pl.BlockSpec(cw.shape, lambda b, j: (0, 0)),
            pl.BlockSpec(cb.shape, lambda b, j: (0, 0)),
        ],
        out_specs=[
            pl.BlockSpec((None, HY_W, seq), lambda b, j: (jnp.minimum(j, 1), 0, b)),
            pl.BlockSpec((seq, half), lambda b, j: (b, jnp.maximum(j - 2, 0))),
        ],
        out_shape=[
            jax.ShapeDtypeStruct((2, HY_W, batch * seq), BF16),
            jax.ShapeDtypeStruct((batch * seq, 2 * half), BF16),
        ],
        scratch_shapes=[pltpu.VMEM((seq, d), BF16)],
        compiler_params=_params(("arbitrary", "arbitrary")),
        name="inproj0",
    )(x2, mod4, g, w4, cw, cb)


def _filter_kernel(w1t_ref, w1c_ref, w1s_ref, b1_ref, f_ref, w2_ref, b2_ref, w3_ref, bias_ref, o_ref, *, seq):
    two_l = o_ref.shape[-1]
    cb = o_ref.shape[0]
    lag = lax.broadcasted_iota(jnp.int32, (1, two_l), 1) - seq
    n = jnp.abs(lag).astype(F32)
    t = n / float(max(seq - 1, 1))
    band_i = lax.broadcasted_iota(jnp.int32, (HY_BANDS, 1), 0).astype(F32)
    bands = 1e-4 + band_i * ((HY_BANDS - 1 - 1e-4) / (HY_BANDS - 1))
    ang = (2.0 * math.pi / seq) * n * bands
    f = f_ref[...]
    pre = (w1t_ref[...] * t
           + jnp.dot(w1c_ref[...], jnp.cos(ang), preferred_element_type=F32, precision=HIGHEST)
           + jnp.dot(w1s_ref[...], -jnp.sin(ang), preferred_element_type=F32, precision=HIGHEST))
    h = jnp.sin(f * (pre + b1_ref[...]))
    h = jnp.sin(f * (jnp.dot(w2_ref[...], h, preferred_element_type=F32, precision=HIGHEST) + b2_ref[...]))
    h_f = jnp.dot(w3_ref[0], h, preferred_element_type=F32, precision=HIGHEST)
    h_b = jnp.dot(w3_ref[1], h, preferred_element_type=F32, precision=HIGHEST)
    max_decay = math.log(HY_DECAY_TARGET) / HY_FAST_PCT
    min_decay = math.log(HY_DECAY_TARGET) / HY_SLOW_PCT
    ch = (lax.broadcasted_iota(jnp.int32, (cb, 1), 0) + pl.program_id(0) * cb).astype(F32)
    deltas = jnp.abs(min_decay + ch * ((max_decay - min_decay) / (HY_W - 1)))
    window = jnp.exp(-t * deltas)
    k = jnp.where(lag >= 0, h_f, h_b) * window
    k = jnp.where(lag == -seq, 0.0, k)
    o_ref[...] = k + jnp.where(lag == 0, bias_ref[...], 0.0)


def _filters(seq, w1, b1, freq, w2, b2, w3, bias):
    cb = 128
    w1t = w1.T
    args = (
        w1t[:, 0:1], w1t[:, 1:1 + HY_BANDS], w1t[:, 1 + HY_BANDS:],
        b1.reshape(HY_ORDER, 1), freq.reshape(HY_ORDER, 1), w2.T, b2.reshape(HY_ORDER, 1),
        w3.T.reshape(2, HY_W, HY_ORDER), bias.reshape(HY_W, 1),
    )
    full = lambda a: pl.BlockSpec(a.shape, lambda i: (0,) * a.ndim)
    in_specs = [full(a) for a in args[:7]] + [
        pl.BlockSpec((2, cb, HY_ORDER), lambda i: (0, i, 0)),
        pl.BlockSpec((cb, 1), lambda i: (i, 0)),
    ]
    return pl.pallas_call(
        functools.partial(_filter_kernel, seq=seq),
        grid=(HY_W // cb,),
        in_specs=in_specs,
        out_specs=pl.BlockSpec((cb, 2 * seq), lambda i: (i, 0)),
        out_shape=jax.ShapeDtypeStruct((HY_W, 2 * seq), F32),
        compiler_params=_params(("arbitrary",)),
        name="hyena_filters",
    )(*args)


def _hyena_kernel(xg_ref, u_ref, kext_ref, out_ref, t_all, u2_all, y2_all, *, seq, batch):
    cblk = out_ref.shape[0]
    nb = seq // TOEP
    half = seq // LANES
    row_i = lax.broadcasted_iota(jnp.int32, (LANES, LANES), 0)
    col_i = lax.broadcasted_iota(jnp.int32, (LANES, LANES), 1)
    upper = col_i >= row_i

    def channel(c, t_ref, u2_ref, y2_ref):
        krow = kext_ref[pl.ds(c, 1), :]

        def rotated(q):
            a = jnp.broadcast_to(krow[:, q * LANES:(q + 1) * LANES], (LANES, LANES))
            return pltpu.roll(a, 0, 1, stride=1, stride_axis=0)

        e_lo = -(2 * nb - 1)
        r_prev = rotated(half + e_lo - 1)
        for e in range(e_lo, 2 * nb):
            r_cur = rotated(half + e)
            g = jnp.where(upper, r_cur, r_prev).astype(BF16)
            r_prev = r_cur
            if e % 2 == 0:
                d = e // 2
                t_ref[d + nb - 1, 0:LANES, 0:LANES] = g
                t_ref[d + nb - 1, LANES:TOEP, LANES:TOEP] = g
            else:
                d = (e - 1) // 2
                if abs(d) <= nb - 1:
                    t_ref[d + nb - 1, 0:LANES, LANES:TOEP] = g
                d = (e + 1) // 2
                if abs(d) <= nb - 1:
                    t_ref[d + nb - 1, LANES:TOEP, 0:LANES] = g
        for j in range(nb):
            u2_ref[j * batch:(j + 1) * batch, :] = u_ref[c, :, j * TOEP:(j + 1) * TOEP]
        y2_ref[...] = _dot(u2_ref[...], t_ref[nb - 1])
        for d in list(range(1, nb)) + list(range(-(nb - 1), 0)):
            j_lo, j_hi = max(0, -d), nb - max(0, d)
            res = _dot(u2_ref[j_lo * batch:j_hi * batch, :], t_ref[d + nb - 1])
            y2_ref[(j_lo + d) * batch:(j_hi + d) * batch, :] += res
        for j in range(nb):
            sl = slice(j * TOEP, (j + 1) * TOEP)
            out_ref[c, :, sl] = (xg_ref[c, :, sl].astype(F32) * y2_ref[j * batch:(j + 1) * batch, :]).astype(BF16)

    def group(i, carry):
        for k in range(HY_PAR):
            channel(i * HY_PAR + k, t_all.at[k], u2_all.at[k], y2_all.at[k])
        return carry

    lax.fori_loop(0, cblk // HY_PAR, group, 0)


def _hyena(hy, kext, batch, seq):
    cblk = 8
    nb = seq // TOEP
    hy4 = hy.reshape(2, HY_W, batch, seq)
    part = lambda p: pl.BlockSpec((None, cblk, batch, seq), lambda i: (p, i, 0, 0))
    return pl.pallas_call(
        functools.partial(_hyena_kernel, seq=seq, batch=batch),
        grid=(HY_W // cblk,),
        in_specs=[part(0), part(1), pl.BlockSpec((cblk, 2 * seq), lambda i: (i, 0))],
        out_specs=pl.BlockSpec((cblk, batch, seq), lambda i: (i, 0, 0)),
        out_shape=jax.ShapeDtypeStruct((HY_W, batch, seq), BF16),
        scratch_shapes=[
            pltpu.VMEM((HY_PAR, 2 * nb - 1, TOEP, TOEP), BF16),
            pltpu.VMEM((HY_PAR, nb * batch, TOEP), BF16),
            pltpu.VMEM((HY_PAR, nb * batch, TOEP), F32),
        ],
        compiler_params=_params(("arbitrary",)),
        name="hyena",
    )(hy4, hy4, kext)


def _retention_tables(af_ref, ab_ref, tab_ref):
    k_scale = RET_HD ** -0.5
    ri = lax.broadcasted_iota(jnp.int32, (CHUNK, CHUNK), 0).astype(F32)
    ci = lax.broadcasted_iota(jnp.int32, (CHUNK, CHUNK), 1).astype(F32)
    diff = ri - ci
    for h in range(RET_H):
        log_f = -jnp.exp(jnp.full((CHUNK, CHUNK), af_ref[h], F32))
        log_b = -jnp.exp(jnp.full((CHUNK, CHUNK), ab_ref[h], F32))
        dec_f = jnp.where(diff >= 0, jnp.exp(log_f * jnp.maximum(diff, 0.0)), 0.0)
        dec_b = jnp.where(diff <= 0, jnp.exp(log_b * jnp.maximum(-diff, 0.0)), 0.0)
        tab_ref[h, T_DEC] = (dec_f + dec_b) * k_scale
        tab_ref[h, T_QDF] = jnp.exp(log_f * (ri + 1.0))
        tab_ref[h, T_QDB] = jnp.exp(log_b * (CHUNK - ri))
        tab_ref[h, T_KDF] = jnp.exp(log_f * (CHUNK - 1.0 - ri)) * k_scale
        tab_ref[h, T_KDB] = jnp.exp(log_b * ri) * k_scale
        tab_ref[h, T_CDF] = jnp.exp(log_f * CHUNK)
        tab_ref[h, T_CDB] = jnp.exp(log_b * CHUNK)


def _retention_kernel(af_ref, ab_ref, zc_ref, zx_ref, ng_ref, oc_ref, ox_ref,
                      tab_ref, sf_ref, sb_ref, sbc_ref, sbx_ref):
    hd = RET_HD

    @pl.when(pl.program_id(0) == 0)
    def _():
        _retention_tables(af_ref, ab_ref, tab_ref)

    sf_ref[...] = jnp.zeros_like(sf_ref)
    sb_ref[...] = jnp.zeros_like(sb_ref)

    def head_cols(part, h):
        return slice((part * RET_H + h) * hd, (part * RET_H + h + 1) * hd)

    def state_update(s, k, v, kd, cd):
        kdt = jnp.transpose(k.astype(F32) * kd).astype(BF16)
        return s * cd + _dot(kdt, v)

    def state_pass(z_ref, store_ref):
        nchunk = z_ref.shape[0] // CHUNK

        def body(i, carry):
            for t, s_ref, lo, t_kd, t_cd in ((i, sf_ref, 0, T_KDF, T_CDF),
                                             (nchunk - 1 - i, sb_ref, hd, T_KDB, T_CDB)):
                rows = pl.ds(pl.multiple_of(t * CHUNK, CHUNK), CHUNK)
                for h in range(RET_H):
                    k = z_ref[rows, head_cols(1, h)]
                    v = z_ref[rows, head_cols(2, h)]
                    s = s_ref[h]
                    store_ref[t, h, :, lo:lo + hd] = s.astype(BF16)
                    s_ref[h] = state_update(s, k, v, tab_ref[h, t_kd], tab_ref[h, t_cd])
            return carry

        lax.fori_loop(0, nchunk, body, 0, unroll=2)

    def out_pass(z_ref, store_ref, out_ref):
        nchunk = z_ref.shape[0] // CHUNK

        def body(t, carry):
            rows = pl.ds(pl.multiple_of(t * CHUNK, CHUNK), CHUNK)
            for h in range(RET_H):
                q = z_ref[rows, head_cols(0, h)]
                k = z_ref[rows, head_cols(1, h)]
                v = z_ref[rows, head_cols(2, h)]
                scores = _dot_nt(q, k) * tab_ref[h, T_DEC]
                cross = _dot(q, store_ref[t, h])
                o = (_dot(scores.astype(BF16), v) + cross[:, :hd] * tab_ref[h, T_QDF]
                     + cross[:, hd:] * tab_ref[h, T_QDB])
                mu = jnp.mean(o, axis=-1, keepdims=True)
                oc = o - mu
                var = jnp.mean(oc * oc, axis=-1, keepdims=True)
                y = oc * lax.rsqrt(var + EPS) * ng_ref[:, h * hd:(h + 1) * hd]
                gate = _silu(z_ref[rows, head_cols(3, h)].astype(F32))
                out_ref[rows, h * hd:(h + 1) * hd] = (y * gate).astype(BF16)
            return carry

        lax.fori_loop(0, nchunk, body, 0, unroll=2)

    state_pass(zc_ref, sbc_ref)
    state_pass(zx_ref, sbx_ref)
    out_pass(zc_ref, sbc_ref, oc_ref)
    out_pass(zx_ref, sbx_ref, ox_ref)


def _retention(zret_c, zret_x, a_f, a_b, norm_g, batch, ctx_len, seq):
    hd = RET_HD
    smem = pl.BlockSpec(memory_space=pltpu.SMEM)
    return pl.pallas_call(
        _retention_kernel,
        grid=(batch,),
        in_specs=[smem, smem,
                  pl.BlockSpec((ctx_len, 4 * RET_W), lambda b: (b, 0)),
                  pl.BlockSpec((seq, 4 * RET_W), lambda b: (b, 0)),
                  pl.BlockSpec((1, RET_W), lambda b: (0, 0))],
        out_specs=[pl.BlockSpec((ctx_len, RET_W), lambda b: (b, 0)),
                   pl.BlockSpec((seq, RET_W), lambda b: (b, 0))],
        out_shape=[jax.ShapeDtypeStruct((batch * ctx_len, RET_W), BF16),
                   jax.ShapeDtypeStruct((batch * seq, RET_W), BF16)],
        scratch_shapes=[
            pltpu.VMEM((RET_H, N_TABLES, CHUNK, CHUNK), F32),
            pltpu.VMEM((RET_H, hd, hd), F32),
            pltpu.VMEM((RET_H, hd, hd), F32),
            pltpu.VMEM((ctx_len // CHUNK, RET_H, hd, 2 * hd), BF16),
            pltpu.VMEM((seq // CHUNK, RET_H, hd, 2 * hd), BF16),
        ],
        compiler_params=_params(("arbitrary",)),
        name="retention",
    )(a_f, a_b, zret_c, zret_x, norm_g.reshape(1, RET_W))


def _head_norm_rope(z, gain, cos, sin):
    ms = jnp.mean(z * z, axis=-1, keepdims=True)
    y = z * lax.rsqrt(ms + EPS) * gain
    if cos is None:
        return y
    return y * cos + pltpu.roll(y, HEAD_DIM // 2, 1) * sin


def _store_v_aug(v_ref, v):
    ones = jnp.ones((v.shape[0], HEAD_DIM), BF16)
    for hd in range(N_KV):
        v_ref[:, 2 * hd * HEAD_DIM:(2 * hd + 1) * HEAD_DIM] = v[:, hd * HEAD_DIM:(hd + 1) * HEAD_DIM].astype(BF16)
        v_ref[:, (2 * hd + 1) * HEAD_DIM:(2 * hd + 2) * HEAD_DIM] = ones


def _mid_kernel(x_ref, yhyt_ref, yret_ref, mod0_ref, mod1_ref, wo_hy_ref, wo_ret_ref, g1_ref, win_ref,
                qn_ref, kn_ref, *rest, latent):
    d = x_ref.shape[-1]
    if latent:
        cos_ref, sin_ref, xo_ref, q_ref, k_ref, v_ref, g_ref = rest
    else:
        k_ref, v_ref = rest
    tm = x_ref.shape[0]
    sub = min(tm, MID_ROWS)
    for r in range(0, tm, sub):
        rs = slice(r, r + sub)
        cos, sin = (cos_ref[rs, :], sin_ref[rs, :]) if latent else (None, None)
        y = _dot_tn(yhyt_ref[:, rs], wo_hy_ref[...]) + _dot(yret_ref[rs, :], wo_ret_ref[...])
        x = x_ref[rs, :] + mod0_ref[:, 2 * d:3 * d] * y
        if latent:
            xo_ref[rs, :] = x
        h = _mod_norm(x, g1_ref[...], mod1_ref[:, 0:d], mod1_ref[:, d:2 * d]).astype(BF16)
        off = 0
        pair = 2 * HEAD_DIM
        if latent:
            for hp in range(N_HEADS // 2):
                z2 = _dot(h, win_ref[:, off:off + pair])
                for i in range(2):
                    q = _head_norm_rope(z2[:, i * HEAD_DIM:(i + 1) * HEAD_DIM], qn_ref[...], cos, sin) * Q_PRESCALE
                    q_ref[rs, off + i * HEAD_DIM:off + (i + 1) * HEAD_DIM] = q.astype(BF16)
                off += pair
        z2 = _dot(h, win_ref[:, off:off + pair])
        for i in range(N_KV):
            k = _head_norm_rope(z2[:, i * HEAD_DIM:(i + 1) * HEAD_DIM], kn_ref[...], cos, sin)
            k_ref[rs, i * HEAD_DIM:(i + 1) * HEAD_DIM] = k.astype(BF16)
        off += pair
        _store_v_aug(v_ref.at[rs, :], _dot(h, win_ref[:, off:off + KV_W]))
        off += KV_W
        if latent:
            g_ref[rs, :] = _dot(h, win_ref[:, off:off + ATT_W]).astype(BF16)


def _mid(x2, yhyt, yret, mod4, mod_row, wo_hy, wo_ret, g1, win, qn, kn, cosf, sinf, batch, seq, tm, latent):
    d = x2.shape[-1]
    nt = seq // tm
    tok = lambda w: pl.BlockSpec((tm, w), lambda b, t: (b * nt + t, 0))
    const = lambda a: pl.BlockSpec(a.shape, lambda b, t: (0,) * a.ndim)
    modspec = lambda layer: pl.BlockSpec((None, None, 1, 3 * d), lambda b, t: (layer, mod_row(b), 0, 0))
    in_specs = [tok(d), pl.BlockSpec((HY_W, tm), lambda b, t: (0, b * nt + t)), tok(RET_W),
                modspec(0), modspec(1), const(wo_hy), const(wo_ret), const(g1), const(win), const(qn), const(kn)]
    args = [x2, yhyt, yret, mod4, mod4, wo_hy, wo_ret, g1, win, qn, kn]
    n = batch * seq
    if latent:
        in_specs += [pl.BlockSpec((tm, HEAD_DIM), lambda b, t: (t, 0))] * 2
        args += [cosf, sinf]
        out_specs = [tok(d), tok(ATT_W), tok(KV_W), tok(2 * KV_W), tok(ATT_W)]
        out_shape = [jax.ShapeDtypeStruct((n, d), F32), jax.ShapeDtypeStruct((n, ATT_W), BF16),
                     jax.ShapeDtypeStruct((n, KV_W), BF16), jax.ShapeDtypeStruct((n, 2 * KV_W), BF16),
                     jax.ShapeDtypeStruct((n, ATT_W), BF16)]
    else:
        out_specs = [tok(KV_W), tok(2 * KV_W)]
        out_shape = [jax.ShapeDtypeStruct((n, KV_W), BF16), jax.ShapeDtypeStruct((n, 2 * KV_W), BF16)]
    return pl.pallas_call(
        functools.partial(_mid_kernel, latent=latent),
        grid=(batch, nt),
        in_specs=in_specs,
        out_specs=out_specs,
        out_shape=out_shape,
        compiler_params=_params(("arbitrary", "arbitrary")),
        name="mid_latent" if latent else "mid_ctx",
    )(*args)


def _attn_kernel(q_ref, kc_ref, vc_ref, kx_ref, vx_ref, g_ref, x_ref, mod_ref, w_ref, fg_ref, out_ref):
    d = x_ref.shape[-1]
    heads = []
    for hd in range(N_HEADS):
        kvh = hd // GROUP
        ksl = slice(kvh * HEAD_DIM, (kvh + 1) * HEAD_DIM)
        vsl = slice(2 * kvh * HEAD_DIM, 2 * (kvh + 1) * HEAD_DIM)
        sl = slice(hd * HEAD_DIM, (hd + 1) * HEAD_DIM)
        q = q_ref[:, sl]
        s_c = _dot_nt(q, kc_ref[:, ksl])
        s_x = _dot_nt(q, kx_ref[:, ksl])
        m = jnp.maximum(jnp.max(s_c, axis=-1, keepdims=True), jnp.max(s_x, axis=-1, keepdims=True))
        p_c = jnp.exp2(s_c - m).astype(BF16)
        p_x = jnp.exp2(s_x - m).astype(BF16)
        o = _dot(p_c, vc_ref[:, vsl]) + _dot(p_x, vx_ref[:, vsl])
        gate = _silu(g_ref[:, sl].astype(F32))
        heads.append((gate * (o[:, :HEAD_DIM] / o[:, HEAD_DIM:])).astype(BF16))
    y = _dot(jnp.concatenate(heads, axis=1), w_ref[...])
    x = x_ref[...] + mod_ref[:, 2 * d:3 * d] * y
    ms = jnp.mean(x * x, axis=-1, keepdims=True)
    out_ref[...] = x * lax.rsqrt(ms + EPS) * fg_ref[...]


def _attention(q, kc, vc, kx, vx, g, x2, mod4, w, fg, batch, ctx_len, seq, tq):
    d = x2.shape[-1]
    nt = seq // tq
    kv = lambda rows, w_: pl.BlockSpec((rows, w_), lambda b, t: (b, 0))
    tok = lambda w_: pl.BlockSpec((tq, w_), lambda b, t: (b * nt + t, 0))
    const = lambda a: pl.BlockSpec(a.shape, lambda b, t: (0,) * a.ndim)
    return pl.pallas_call(
        _attn_kernel,
        grid=(batch, nt),
        in_specs=[tok(ATT_W), kv(ctx_len, KV_W), kv(ctx_len, 2 * KV_W), kv(seq, KV_W), kv(seq, 2 * KV_W),
                  tok(ATT_W), tok(d),
                  pl.BlockSpec((None, None, 1, 3 * d), lambda b, t: (1, b, 0, 0)), const(w), const(fg)],
        out_specs=tok(d),
        out_shape=jax.ShapeDtypeStruct((batch * seq, d), F32),
        compiler_params=_params(("arbitrary", "arbitrary")),
        name="attention",
    )(q, kc, vc, kx, vx, g, x2, mod4, w, fg)


def _rope_tables(seq):
    rows = seq // GRID_W
    r = jnp.repeat(jnp.arange(rows, dtype=F32), GRID_W)
    col = jnp.tile(jnp.arange(GRID_W, dtype=F32), rows)
    half = HEAD_DIM // 2
    inv = ROPE_THETA ** (-jnp.arange(0, half, 2, dtype=F32) / half)
    ang = jnp.concatenate([r[:, None] * inv, col[:, None] * inv], axis=-1)
    cos, sin = jnp.cos(ang), jnp.sin(ang)
    return jnp.concatenate([cos, cos], axis=-1), jnp.concatenate([-sin, sin], axis=-1)


def kernel(x, c, ctx, c_ctx, norm_g, ada_w, ada_b, er_in_w, er_out_w, hy_conv_w, hy_conv_b, hy_f_w1, hy_f_b1, hy_f_freq, hy_f_w2, hy_f_b2, hy_f_w3, hy_bias, ret_decay_f, ret_decay_b, ret_norm_g, at_in_w, at_out_w, at_q_norm, at_k_norm, final_norm_g):
    batch, seq, d = x.shape
    ctx_len = ctx.shape[1]
    assert seq % TOEP == 0 and ctx_len % TOEP == 0 and seq % GRID_W == 0 and d == 2 * HY_W
    tm = min(512, seq)

    rows = -(-(batch + 1) // 8) * 8
    cond = jnp.concatenate([c, c_ctx[None, :], jnp.zeros((rows - batch - 1, d), F32)], axis=0)
    mod4 = _ada(cond, ada_w, ada_b).reshape(ada_w.shape[0], rows, 1, 3 * d)
    lat_row = lambda b: b
    ctx_row = lambda b: batch

    x2 = x.reshape(batch * seq, d)
    c2 = ctx.reshape(batch * ctx_len, d)

    in_w = er_in_w[0].astype(BF16)
    col = lambda p: in_w[:, p * HY_W:(p + 1) * HY_W].T
    w4 = jnp.stack([jnp.concatenate([col(0), col(3)], axis=0), jnp.concatenate([col(1), col(2)], axis=0),
                    in_w[:, 4 * HY_W:4 * HY_W + d], in_w[:, 4 * HY_W + d:]])
    cw = hy_conv_w[0].T
    cb = hy_conv_b[0].reshape(3 * HY_W, 1)
    g0 = norm_g[0].reshape(1, d)
    hy_x, zret_x = _inproj0(x2, mod4, 0, lat_row, g0, w4, cw, cb, batch, seq, seq)
    ctx_rows = math.gcd(batch * ctx_len, seq)
    hy_c, zret_c = _inproj0(c2, mod4, 0, ctx_row, g0, w4, cw, cb, batch * ctx_len // ctx_rows, ctx_rows, ctx_len)

    filt = (hy_f_w1[0], hy_f_b1[0], hy_f_freq[0], hy_f_w2[0], hy_f_b2[0], hy_f_w3[0], hy_bias[0])
    yhyt_x = _hyena(hy_x, _filters(seq, *filt), batch, seq)
    yhyt_c = _hyena(hy_c, _filters(ctx_len, *filt), batch, ctx_len)
    yret_c, yret_x = _retention(zret_c, zret_x, ret_decay_f[0], ret_decay_b[0], ret_norm_g[0], batch, ctx_len, seq)

    wo_hy = er_out_w[0][:HY_W].astype(BF16)
    wo_ret = er_out_w[0][HY_W:].astype(BF16)
    perm = jnp.concatenate([jnp.arange(0, HEAD_DIM, 2), jnp.arange(1, HEAD_DIM, 2)])
    qk_cols = (jnp.arange(N_HEADS + N_KV)[:, None] * HEAD_DIM + perm[None, :]).reshape(-1)
    w1 = at_in_w[0]
    win_x = jnp.concatenate([w1[:, qk_cols], w1[:, ATT_W + KV_W:]], axis=1).astype(BF16)
    win_c = win_x[:, ATT_W:ATT_W + 2 * KV_W]
    qn = at_q_norm[0][perm].reshape(1, HEAD_DIM)
    kn = at_k_norm[0][perm].reshape(1, HEAD_DIM)
    g1 = norm_g[1].reshape(1, d)
    cosf, sinf = _rope_tables(seq)
    x1, q, kx, vx, gate = _mid(x2, yhyt_x.reshape(HY_W, batch * seq), yret_x, mod4, lat_row, wo_hy, wo_ret, g1,
                               win_x, qn, kn, cosf, sinf, batch, seq, tm, True)
    kc, vc = _mid(c2, yhyt_c.reshape(HY_W, batch * ctx_len), yret_c, mod4, ctx_row, wo_hy, wo_ret, g1,
                  win_c, qn, kn, None, None, batch, ctx_len, ctx_len, False)

    out = _attention(q, kc, vc, kx, vx, gate, x1, mod4, at_out_w[0].astype(BF16), final_norm_g.reshape(1, d),
                     batch, ctx_len, seq, min(512, seq))
    return out.reshape(batch, seq, d)
```

```python
import functools
import math

import jax
import jax.numpy as jnp
from jax import lax
from jax.experimental import pallas as pl
from jax.experimental.pallas import tpu as pltpu

F32 = jnp.float32
BF16 = jnp.bfloat16
HIGHEST = lax.Precision.HIGHEST

EPS = 1e-6
GRID_W = 64
HY_W = 512
HY_EMB = 33
HY_BANDS = (HY_EMB - 1) // 2
HY_ORDER = 64
HY_DECAY_TARGET = 1e-2
HY_FAST_PCT = 0.3
HY_SLOW_PCT = 1.5
RET_W = 512
RET_H = 4
RET_HD = RET_W // RET_H
CHUNK = 128
N_HEADS = 8
N_KV = 2
HEAD_DIM = 128
GROUP = N_HEADS // N_KV
ATT_W = N_HEADS * HEAD_DIM
KV_W = N_KV * HEAD_DIM
ROPE_THETA = 10000.0
Q_PRESCALE = HEAD_DIM ** -0.5 * math.log2(math.e)

MID_ROWS = 256
HY_CH = 512
LANES = 128
TOEP = 256
VMEM_LIMIT = 56 * 1024 * 1024
T_DEC, T_QDF, T_QDB, T_KDF, T_KDB, T_CDF, T_CDB = range(7)
N_TABLES = 7


def _params(sem):
    return pltpu.CompilerParams(dimension_semantics=sem, vmem_limit_bytes=VMEM_LIMIT)


def _silu(x):
    return x * (1.0 / (1.0 + jnp.exp(-x)))


def _dot(a, b):
    return jnp.dot(a, b, preferred_element_type=F32)


def _dot_nt(a, b):
    return lax.dot_general(a, b, (((1,), (1,)), ((), ())), preferred_element_type=F32)


def _dot_tn(a, b):
    return lax.dot_general(a, b, (((0,), (0,)), ((), ())), preferred_element_type=F32)


def _mod_norm(x, g, shift, scale):
    ms = jnp.mean(x * x, axis=-1, keepdims=True)
    return (x * lax.rsqrt(ms + EPS) * g) * (1.0 + scale) + shift


def _ada_kernel(cond_ref, w_ref, b_ref, o_ref):
    s = _silu(cond_ref[...])
    o_ref[...] = jnp.dot(s, w_ref[...], preferred_element_type=F32, precision=HIGHEST) + b_ref[...]


def _ada(cond, ada_w, ada_b):
    depth, d, n = ada_w.shape
    rows = cond.shape[0]
    tn = 1024
    return pl.pallas_call(
        _ada_kernel,
        grid=(depth, n // tn),
        in_specs=[
            pl.BlockSpec((rows, d), lambda i, j: (0, 0)),
            pl.BlockSpec((None, d, tn), lambda i, j: (i, 0, j)),
            pl.BlockSpec((None, 1, tn), lambda i, j: (i, 0, j)),
        ],
        out_specs=pl.BlockSpec((None, rows, tn), lambda i, j: (i, 0, j)),
        out_shape=jax.ShapeDtypeStruct((depth, rows, n), F32),
        compiler_params=_params(("arbitrary", "arbitrary")),
        name="ada",
    )(cond, ada_w, ada_b.reshape(depth, 1, n))


def _inproj0_kernel(x_ref, mod_ref, g_ref, w_ref, cw_ref, cb_ref, hy_ref, zret_ref, h_ref, *, period):
    j = pl.program_id(1)
    seq, d = x_ref.shape
    rows = min(seq, 2 * MID_ROWS)

    @pl.when(j == 0)
    def _():
        for r in range(0, seq, rows):
            rs = slice(r, r + rows)
            h_ref[rs, :] = _mod_norm(x_ref[rs, :], g_ref[...], mod_ref[:, 0:d], mod_ref[:, d:2 * d]).astype(BF16)

    lane = lax.broadcasted_iota(jnp.int32, (1, seq), 1)
    first = lane % period == 0
    last = lane % period == period - 1

    def conv(a, part, c0):
        ch = slice(part * HY_W + c0, part * HY_W + c0 + a.shape[0])
        left = jnp.where(first, 0.0, pltpu.roll(a, 1, 1))
        right = jnp.where(last, 0.0, pltpu.roll(a, seq - 1, 1))
        return left * cw_ref[ch, 0:1] + a * cw_ref[ch, 1:2] + right * cw_ref[ch, 2:3] + cb_ref[ch, :]

    def hyena_step(combine):
        h = h_ref[...]
        sub = 8
        for c0 in range(0, HY_W, HY_CH):
            za = _dot_nt(w_ref[c0:c0 + HY_CH, :], h)
            zb = _dot_nt(w_ref[HY_W + c0:HY_W + c0 + HY_CH, :], h)
            for r in range(0, HY_CH, 2 * sub):
                pieces = [combine(za[r + i:r + i + sub, :], zb[r + i:r + i + sub, :], c0 + r + i) for i in (0, sub)]
                hy_ref[c0 + r:c0 + r + 2 * sub, :] = jnp.concatenate(pieces, axis=0).astype(BF16)

    @pl.when(j == 0)
    def _():
        hyena_step(lambda x0, gate, c0: conv(x0, 0, c0) * _silu(gate))

    @pl.when(j == 1)
    def _():
        hyena_step(lambda x1, v, c0: conv(x1, 1, c0) * conv(v, 2, c0))

    @pl.when(j >= 2)
    def _():
        for r in range(0, seq, rows):
            rs = slice(r, r + rows)
            zret_ref[rs, :] = _dot(h_ref[rs, :], w_ref[...]).astype(BF16)


def _inproj0(x2, mod4, layer, mod_row, g, w4, cw, cb, batch, seq, period):
    d = x2.shape[-1]
    half = 2 * HY_W
    return pl.pallas_call(
        functools.partial(_inproj0_kernel, period=period),
        grid=(batch, 4),
        in_specs=[
            pl.BlockSpec((seq, d), lambda b, j: (b, 0), pipeline_mode=pl.Buffered(1)),
            pl.BlockSpec((None, None, 1, 3 * d), lambda b, j: (layer, mod_row(b), 0, 0)),
            pl.BlockSpec((1, d), lambda b, j: (0, 0)),
            pl.BlockSpec((None, half, d), lambda b, j: (j, 0, 0)),
            pl.BlockSpec(cw.shape, lambda b, j: (0, 0)),
            pl.BlockSpec(cb.shape, lambda b, j: (0, 0)),
        ],
        out_specs=[
            pl.BlockSpec((None, HY_W, seq), lambda b, j: (jnp.minimum(j, 1), 0, b)),
            pl.BlockSpec((seq, half), lambda b, j: (b, jnp.maximum(j - 2, 0))),
        ],
        out_shape=[
            jax.ShapeDtypeStruct((2, HY_W, batch * seq), BF16),
            jax.ShapeDtypeStruct((batch * seq, 2 * half), BF16),
        ],
        scratch_shapes=[pltpu.VMEM((seq, d), BF16)],
        compiler_params=_params(("arbitrary", "arbitrary")),
        name="inproj0",
    )(x2, mod4, g, w4, cw, cb)


def _filter_kernel(w1t_ref, w1c_ref, w1s_ref, b1_ref, f_ref, w2_ref, b2_ref, w3_ref, bias_ref, o_ref, *, seq):
    two_l = o_ref.shape[-1]
    cb = o_ref.shape[0]
    lag = lax.broadcasted_iota(jnp.int32, (1, two_l), 1) - seq
    n = jnp.abs(lag).astype(F32)
    t = n / float(max(seq - 1, 1))
    band_i = lax.broadcasted_iota(jnp.int32, (HY_BANDS, 1), 0).astype(F32)
    bands = 1e-4 + band_i * ((HY_BANDS - 1 - 1e-4) / (HY_BANDS - 1))
    ang = (2.0 * math.pi / seq) * n * bands
    f = f_ref[...]
    pre = (w1t_ref[...] * t
           + jnp.dot(w1c_ref[...], jnp.cos(ang), preferred_element_type=F32, precision=HIGHEST)
           + jnp.dot(w1s_ref[...], -jnp.sin(ang), preferred_element_type=F32, precision=HIGHEST))
    h = jnp.sin(f * (pre + b1_ref[...]))
    h = jnp.sin(f * (jnp.dot(w2_ref[...], h, preferred_element_type=F32, precision=HIGHEST) + b2_ref[...]))
    h_f = jnp.dot(w3_ref[0], h, preferred_element_type=F32, precision=HIGHEST)
    h_b = jnp.dot(w3_ref[1], h, preferred_element_type=F32, precision=HIGHEST)
    max_decay = math.log(HY_DECAY_TARGET) / HY_FAST_PCT
    min_decay = math.log(HY_DECAY_TARGET) / HY_SLOW_PCT
    ch = (lax.broadcasted_iota(jnp.int32, (cb, 1), 0) + pl.program_id(0) * cb).astype(F32)
    deltas = jnp.abs(min_decay + ch * ((max_decay - min_decay) / (HY_W - 1)))
    window = jnp.exp(-t * deltas)
    k = jnp.where(lag >= 0, h_f, h_b) * window
    k = jnp.where(lag == -seq, 0.0, k)
    o_ref[...] = k + jnp.where(lag == 0, bias_ref[...], 0.0)


def _filters(seq, w1, b1, freq, w2, b2, w3, bias):
    cb = 128
    w1t = w1.T
    args = (
        w1t[:, 0:1], w1t[:, 1:1 + HY_BANDS], w1t[:, 1 + HY_BANDS:],
        b1.reshape(HY_ORDER, 1), freq.reshape(HY_ORDER, 1), w2.T, b2.reshape(HY_ORDER, 1),
        w3.T.reshape(2, HY_W, HY_ORDER), bias.reshape(HY_W, 1),
    )
    full = lambda a: pl.BlockSpec(a.shape, lambda i: (0,) * a.ndim)
    in_specs = [full(a) for a in args[:7]] + [
        pl.BlockSpec((2, cb, HY_ORDER), lambda i: (0, i, 0)),
        pl.BlockSpec((cb, 1), lambda i: (i, 0)),
    ]
    return pl.pallas_call(
        functools.partial(_filter_kernel, seq=seq),
        grid=(HY_W // cb,),
        in_specs=in_specs,
        out_specs=pl.BlockSpec((cb, 2 * seq), lambda i: (i, 0)),
        out_shape=jax.ShapeDtypeStruct((HY_W, 2 * seq), F32),
        compiler_params=_params(("arbitrary",)),
        name="hyena_filters",
    )(*args)


def _hyena_kernel(xg_ref, u_ref, kext_ref, out_ref, t_all, u2_all, y2_all, *, seq, batch):
    cblk = out_ref.shape[0]
    nb = seq // TOEP
    half = seq // LANES
    row_i = lax.broadcasted_iota(jnp.int32, (LANES, LANES), 0)
    col_i = lax.broadcasted_iota(jnp.int32, (LANES, LANES), 1)
    upper = col_i >= row_i

    def channel(c, t_ref, u2_ref, y2_ref):
        krow = kext_ref[c]

        def rotated(q):
            a = jnp.broadcast_to(krow[:, q * LANES:(q + 1) * LANES], (LANES, LANES))
            return pltpu.roll(a, 0, 1, stride=1, stride_axis=0)

        e_lo = -(2 * nb - 1)
        r_prev = rotated(half + e_lo - 1)
        for e in range(e_lo, 2 * nb):
            r_cur = rotated(half + e)
            g = jnp.where(upper, r_cur, r_prev).astype(BF16)
            r_prev = r_cur
            if e % 2 == 0:
                d = e // 2
                t_ref[d + nb - 1, 0:LANES, 0:LANES] = g
                t_ref[d + nb - 1, LANES:TOEP, LANES:TOEP] = g
            else:
                d = (e - 1) // 2
                if abs(d) <= nb - 1:
                    t_ref[d + nb - 1, 0:LANES, LANES:TOEP] = g
                d = (e + 1) // 2
                if abs(d) <= nb - 1:
                    t_ref[d + nb - 1, LANES:TOEP, 0:LANES] = g
        for j in range(nb):
            u2_ref[j * batch:(j + 1) * batch, :] = u_ref[c, :, j * TOEP:(j + 1) * TOEP]
        y2_ref[...] = _dot(u2_ref[...], t_ref[nb - 1])
        for d in list(range(1, nb)) + list(range(-(nb - 1), 0)):
            j_lo, j_hi = max(0, -d), nb - max(0, d)
            res = _dot(u2_ref[j_lo * batch:j_hi * batch, :], t_ref[d + nb - 1])
            y2_ref[(j_lo + d) * batch:(j_hi + d) * batch, :] += res
        for j in range(nb):
            sl = slice(j * TOEP, (j + 1) * TOEP)
            out_ref[c, :, sl] = (xg_ref[c, :, sl].astype(F32) * y2_ref[j * batch:(j + 1) * batch, :]).astype(BF16)

    for c in range(cblk):
        channel(c, t_all.at[c], u2_all.at[c], y2_all.at[c])


def _hyena(hy, kext, batch, seq):
    cblk = 8
    nb = seq // TOEP
    hy4 = hy.reshape(2, HY_W, batch, seq)
    part = lambda p: pl.BlockSpec((None, cblk, batch, seq), lambda i: (p, i, 0, 0))
    return pl.pallas_call(
        functools.partial(_hyena_kernel, seq=seq, batch=batch),
        grid=(HY_W // cblk,),
        in_specs=[part(0), part(1), pl.BlockSpec((cblk, 1, 2 * seq), lambda i: (i, 0, 0))],
        out_specs=pl.BlockSpec((cblk, batch, seq), lambda i: (i, 0, 0)),
        out_shape=jax.ShapeDtypeStruct((HY_W, batch, seq), BF16),
        scratch_shapes=[
            pltpu.VMEM((cblk, 2 * nb - 1, TOEP, TOEP), BF16),
            pltpu.VMEM((cblk, nb * batch, TOEP), BF16),
            pltpu.VMEM((cblk, nb * batch, TOEP), F32),
        ],
        compiler_params=_params(("arbitrary",)),
        name="hyena",
    )(hy4, hy4, kext.reshape(HY_W, 1, 2 * seq))


def _retention_tables_kernel(af_ref, ab_ref, tab_ref):
    k_scale = RET_HD ** -0.5
    ri = lax.broadcasted_iota(jnp.int32, (CHUNK, CHUNK), 0).astype(F32)
    ci = lax.broadcasted_iota(jnp.int32, (CHUNK, CHUNK), 1).astype(F32)
    diff = ri - ci
    for h in range(RET_H):
        log_f = -jnp.exp(jnp.full((CHUNK, CHUNK), af_ref[h], F32))
        log_b = -jnp.exp(jnp.full((CHUNK, CHUNK), ab_ref[h], F32))
        dec_f = jnp.where(diff >= 0, jnp.exp(log_f * jnp.maximum(diff, 0.0)), 0.0)
        dec_b = jnp.where(diff <= 0, jnp.exp(log_b * jnp.maximum(-diff, 0.0)), 0.0)
        tab_ref[h, T_DEC] = (dec_f + dec_b) * k_scale
        tab_ref[h, T_QDF] = jnp.exp(log_f * (ri + 1.0))
        tab_ref[h, T_QDB] = jnp.exp(log_b * (CHUNK - ri))
        tab_ref[h, T_KDF] = jnp.exp(log_f * (CHUNK - 1.0 - ri)) * k_scale
        tab_ref[h, T_KDB] = jnp.exp(log_b * ri) * k_scale
        tab_ref[h, T_CDF] = jnp.exp(log_f * CHUNK)
        tab_ref[h, T_CDB] = jnp.exp(log_b * CHUNK)


def _retention_tables(a_f, a_b):
    smem = pl.BlockSpec(memory_space=pltpu.SMEM)
    return pl.pallas_call(
        _retention_tables_kernel,
        in_specs=[smem, smem],
        out_shape=jax.ShapeDtypeStruct((RET_H, N_TABLES, CHUNK, CHUNK), F32),
        name="retention_tables",
    )(a_f, a_b)


def _retention_kernel(tab_ref, zc_ref, zx_ref, ng_ref, oc_ref, ox_ref, sf_ref, sb_ref, sbc_ref, sbx_ref):
    hd = RET_HD
    sf_ref[...] = jnp.zeros_like(sf_ref)
    sb_ref[...] = jnp.zeros_like(sb_ref)

    def head_cols(part, h):
        return slice((part * RET_H + h) * hd, (part * RET_H + h + 1) * hd)

    def state_update(s, k, v, kd, cd):
        kdt = jnp.transpose(k.astype(F32) * kd).astype(BF16)
        return s * cd + _dot(kdt, v)

    def state_pass(z_ref, store_ref):
        nchunk = z_ref.shape[0] // CHUNK

        for i in range(nchunk):
            for t, s_ref, lo, t_kd, t_cd in ((i, sf_ref, 0, T_KDF, T_CDF),
                                             (nchunk - 1 - i, sb_ref, hd, T_KDB, T_CDB)):
                rows = slice(t * CHUNK, (t + 1) * CHUNK)
                for h in range(RET_H):
                    k = z_ref[rows, head_cols(1, h)]
                    v = z_ref[rows, head_cols(2, h)]
                    s = s_ref[h]
                    store_ref[t, h, :, lo:lo + hd] = s.astype(BF16)
                    s_ref[h] = state_update(s, k, v, tab_ref[h, t_kd], tab_ref[h, t_cd])

    def out_pass(z_ref, store_ref, out_ref):
        nchunk = z_ref.shape[0] // CHUNK

        for t in range(nchunk):
            rows = slice(t * CHUNK, (t + 1) * CHUNK)
            for h in range(RET_H):
                q = z_ref[rows, head_cols(0, h)]
                k = z_ref[rows, head_cols(1, h)]
                v = z_ref[rows, head_cols(2, h)]
                scores = _dot_nt(q, k) * tab_ref[h, T_DEC]
                cross = _dot(q, store_ref[t, h])
                o = (_dot(scores.astype(BF16), v) + cross[:, :hd] * tab_ref[h, T_QDF]
                     + cross[:, hd:] * tab_ref[h, T_QDB])
                mu = jnp.mean(o, axis=-1, keepdims=True)
                oc = o - mu
                var = jnp.mean(oc * oc, axis=-1, keepdims=True)
                y = oc * lax.rsqrt(var + EPS) * ng_ref[:, h * hd:(h + 1) * hd]
                gate = _silu(z_ref[rows, head_cols(3, h)].astype(F32))
                out_ref[rows, h * hd:(h + 1) * hd] = (y * gate).astype(BF16)

    state_pass(zc_ref, sbc_ref)
    state_pass(zx_ref, sbx_ref)
    out_pass(zc_ref, sbc_ref, oc_ref)
    out_pass(zx_ref, sbx_ref, ox_ref)


def _retention(zret_c, zret_x, a_f, a_b, norm_g, batch, ctx_len, seq):
    hd = RET_HD
    tables = _retention_tables(a_f, a_b)
    return pl.pallas_call(
        _retention_kernel,
        grid=(batch,),
        in_specs=[pl.BlockSpec(tables.shape, lambda b: (0, 0, 0, 0)),
                  pl.BlockSpec((ctx_len, 4 * RET_W), lambda b: (b, 0)),
                  pl.BlockSpec((seq, 4 * RET_W), lambda b: (b, 0)),
                  pl.BlockSpec((1, RET_W), lambda b: (0, 0))],
        out_specs=[pl.BlockSpec((ctx_len, RET_W), lambda b: (b, 0)),
                   pl.BlockSpec((seq, RET_W), lambda b: (b, 0))],
        out_shape=[jax.ShapeDtypeStruct((batch * ctx_len, RET_W), BF16),
                   jax.ShapeDtypeStruct((batch * seq, RET_W), BF16)],
        scratch_shapes=[
            pltpu.VMEM((RET_H, hd, hd), F32),
            pltpu.VMEM((RET_H, hd, hd), F32),
            pltpu.VMEM((ctx_len // CHUNK, RET_H, hd, 2 * hd), BF16),
            pltpu.VMEM((seq // CHUNK, RET_H, hd, 2 * hd), BF16),
        ],
        compiler_params=_params(("arbitrary",)),
        name="retention",
    )(tables, zret_c, zret_x, norm_g.reshape(1, RET_W))


def _head_norm_rope(z, gain, cos, sin):
    ms = jnp.mean(z * z, axis=-1, keepdims=True)
    y = z * lax.rsqrt(ms + EPS) * gain
    if cos is None:
        return y
    return y * cos + pltpu.roll(y, HEAD_DIM // 2, 1) * sin


def _store_v_aug(v_ref, v):
    ones = jnp.ones((v.shape[0], HEAD_DIM), BF16)
    for hd in range(N_KV):
        v_ref[:, 2 * hd * HEAD_DIM:(2 * hd + 1) * HEAD_DIM] = v[:, hd * HEAD_DIM:(hd + 1) * HEAD_DIM].astype(BF16)
        v_ref[:, (2 * hd + 1) * HEAD_DIM:(2 * hd + 2) * HEAD_DIM] = ones


def _mid_kernel(x_ref, yhyt_ref, yret_ref, mod0_ref, mod1_ref, wo_hy_ref, wo_ret_ref, g1_ref, win_ref,
                qn_ref, kn_ref, *rest, latent):
    d = x_ref.shape[-1]
    if latent:
        cos_ref, sin_ref, xo_ref, q_ref, k_ref, v_ref, g_ref = rest
    else:
        k_ref, v_ref = rest
    tm = x_ref.shape[0]
    sub = min(tm, MID_ROWS)
    for r in range(0, tm, sub):
        rs = slice(r, r + sub)
        cos, sin = (cos_ref[rs, :], sin_ref[rs, :]) if latent else (None, None)
        y = _dot_tn(yhyt_ref[:, rs], wo_hy_ref[...]) + _dot(yret_ref[rs, :], wo_ret_ref[...])
        x = x_ref[rs, :] + mod0_ref[:, 2 * d:3 * d] * y
        if latent:
            xo_ref[rs, :] = x
        h = _mod_norm(x, g1_ref[...], mod1_ref[:, 0:d], mod1_ref[:, d:2 * d]).astype(BF16)
        off = 0
        pair = 2 * HEAD_DIM
        if latent:
            for hp in range(N_HEADS // 2):
                z2 = _dot(h, win_ref[:, off:off + pair])
                for i in range(2):
                    q = _head_norm_rope(z2[:, i * HEAD_DIM:(i + 1) * HEAD_DIM], qn_ref[...], cos, sin) * Q_PRESCALE
                    q_ref[rs, off + i * HEAD_DIM:off + (i + 1) * HEAD_DIM] = q.astype(BF16)
                off += pair
        z2 = _dot(h, win_ref[:, off:off + pair])
        for i in range(N_KV):
            k = _head_norm_rope(z2[:, i * HEAD_DIM:(i + 1) * HEAD_DIM], kn_ref[...], cos, sin)
            k_ref[rs, i * HEAD_DIM:(i + 1) * HEAD_DIM] = k.astype(BF16)
        off += pair
        _store_v_aug(v_ref.at[rs, :], _dot(h, win_ref[:, off:off + KV_W]))
        off += KV_W
        if latent:
            g_ref[rs, :] = _dot(h, win_ref[:, off:off + ATT_W]).astype(BF16)


def _mid(x2, yhyt, yret, mod4, mod_row, wo_hy, wo_ret, g1, win, qn, kn, cosf, sinf, batch, seq, tm, latent):
    d = x2.shape[-1]
    nt = seq // tm
    tok = lambda w: pl.BlockSpec((tm, w), lambda b, t: (b * nt + t, 0))
    const = lambda a: pl.BlockSpec(a.shape, lambda b, t: (0,) * a.ndim)
    modspec = lambda layer: pl.BlockSpec((None, None, 1, 3 * d), lambda b, t: (layer, mod_row(b), 0, 0))
    in_specs = [tok(d), pl.BlockSpec((HY_W, tm), lambda b, t: (0, b * nt + t)), tok(RET_W),
                modspec(0), modspec(1), const(wo_hy), const(wo_ret), const(g1), const(win), const(qn), const(kn)]
    args = [x2, yhyt, yret, mod4, mod4, wo_hy, wo_ret, g1, win, qn, kn]
    n = batch * seq
    if latent:
        in_specs += [pl.BlockSpec((tm, HEAD_DIM), lambda b, t: (t, 0))] * 2
        args += [cosf, sinf]
        out_specs = [tok(d), tok(ATT_W), tok(KV_W), tok(2 * KV_W), tok(ATT_W)]
        out_shape = [jax.ShapeDtypeStruct((n, d), F32), jax.ShapeDtypeStruct((n, ATT_W), BF16),
                     jax.ShapeDtypeStruct((n, KV_W), BF16), jax.ShapeDtypeStruct((n, 2 * KV_W), BF16),
                     jax.ShapeDtypeStruct((n, ATT_W), BF16)]
    else:
        out_specs = [tok(KV_W), tok(2 * KV_W)]
        out_shape = [jax.ShapeDtypeStruct((n, KV_W), BF16), jax.ShapeDtypeStruct((n, 2 * KV_W), BF16)]
    return pl.pallas_call(
        functools.partial(_mid_kernel, latent=latent),
        grid=(batch, nt),
        in_specs=in_specs,
        out_specs=out_specs,
        out_shape=out_shape,
        compiler_params=_params(("arbitrary", "arbitrary")),
        name="mid_latent" if latent else "mid_ctx",
    )(*args)


def _attn_kernel(q_ref, kc_ref, vc_ref, kx_ref, vx_ref, g_ref, x_ref, mod_ref, w_ref, fg_ref, out_ref):
    d = x_ref.shape[-1]
    heads = []
    for hd in range(N_HEADS):
        kvh = hd // GROUP
        ksl = slice(kvh * HEAD_DIM, (kvh + 1) * HEAD_DIM)
        vsl = slice(2 * kvh * HEAD_DIM, 2 * (kvh + 1) * HEAD_DIM)
        sl = slice(hd * HEAD_DIM, (hd + 1) * HEAD_DIM)
        q = q_ref[:, sl]
        s_c = _dot_nt(q, kc_ref[:, ksl])
        s_x = _dot_nt(q, kx_ref[:, ksl])
        m = jnp.maximum(jnp.max(s_c, axis=-1, keepdims=True), jnp.max(s_x, axis=-1, keepdims=True))
        p_c = jnp.exp2(s_c - m).astype(BF16)
        p_x = jnp.exp2(s_x - m).astype(BF16)
        o = _dot(p_c, vc_ref[:, vsl]) + _dot(p_x, vx_ref[:, vsl])
        gate = _silu(g_ref[:, sl].astype(F32))
        heads.append((gate * (o[:, :HEAD_DIM] / o[:, HEAD_DIM:])).astype(BF16))
    y = _dot(jnp.concatenate(heads, axis=1), w_ref[...])
    x = x_ref[...] + mod_ref[:, 2 * d:3 * d] * y
    ms = jnp.mean(x * x, axis=-1, keepdims=True)
    out_ref[...] = x * lax.rsqrt(ms + EPS) * fg_ref[...]


def _attention(q, kc, vc, kx, vx, g, x2, mod4, w, fg, batch, ctx_len, seq, tq):
    d = x2.shape[-1]
    nt = seq // tq
    kv = lambda rows, w_: pl.BlockSpec((rows, w_), lambda b, t: (b, 0))
    tok = lambda w_: pl.BlockSpec((tq, w_), lambda b, t: (b * nt + t, 0))
    const = lambda a: pl.BlockSpec(a.shape, lambda b, t: (0,) * a.ndim)
    return pl.pallas_call(
        _attn_kernel,
        grid=(batch, nt),
        in_specs=[tok(ATT_W), kv(ctx_len, KV_W), kv(ctx_len, 2 * KV_W), kv(seq, KV_W), kv(seq, 2 * KV_W),
                  tok(ATT_W), tok(d),
                  pl.BlockSpec((None, None, 1, 3 * d), lambda b, t: (1, b, 0, 0)), const(w), const(fg)],
        out_specs=tok(d),
        out_shape=jax.ShapeDtypeStruct((batch * seq, d), F32),
        compiler_params=_params(("arbitrary", "arbitrary")),
        name="attention",
    )(q, kc, vc, kx, vx, g, x2, mod4, w, fg)


def _rope_tables(seq):
    rows = seq // GRID_W
    r = jnp.repeat(jnp.arange(rows, dtype=F32), GRID_W)
    col = jnp.tile(jnp.arange(GRID_W, dtype=F32), rows)
    half = HEAD_DIM // 2
    inv = ROPE_THETA ** (-jnp.arange(0, half, 2, dtype=F32) / half)
    ang = jnp.concatenate([r[:, None] * inv, col[:, None] * inv], axis=-1)
    cos, sin = jnp.cos(ang), jnp.sin(ang)
    return jnp.concatenate([cos, cos], axis=-1), jnp.concatenate([-sin, sin], axis=-1)


def kernel(x, c, ctx, c_ctx, norm_g, ada_w, ada_b, er_in_w, er_out_w, hy_conv_w, hy_conv_b, hy_f_w1, hy_f_b1, hy_f_freq, hy_f_w2, hy_f_b2, hy_f_w3, hy_bias, ret_decay_f, ret_decay_b, ret_norm_g, at_in_w, at_out_w, at_q_norm, at_k_norm, final_norm_g):
    batch, seq, d = x.shape
    ctx_len = ctx.shape[1]
    assert seq % TOEP == 0 and ctx_len % TOEP == 0 and seq % GRID_W == 0 and d == 2 * HY_W
    tm = min(512, seq)

    rows = -(-(batch + 1) // 8) * 8
    cond = jnp.concatenate([c, c_ctx[None, :], jnp.zeros((rows - batch - 1, d), F32)], axis=0)
    mod4 = _ada(cond, ada_w, ada_b).reshape(ada_w.shape[0], rows, 1, 3 * d)
    lat_row = lambda b: b
    ctx_row = lambda b: batch

    x2 = x.reshape(batch * seq, d)
    c2 = ctx.reshape(batch * ctx_len, d)

    in_w = er_in_w[0].astype(BF16)
    col = lambda p: in_w[:, p * HY_W:(p + 1) * HY_W].T
    w4 = jnp.stack([jnp.concatenate([col(0), col(3)], axis=0), jnp.concatenate([col(1), col(2)], axis=0),
                    in_w[:, 4 * HY_W:4 * HY_W + d], in_w[:, 4 * HY_W + d:]])
    cw = hy_conv_w[0].T
    cb = hy_conv_b[0].reshape(3 * HY_W, 1)
    g0 = norm_g[0].reshape(1, d)
    hy_x, zret_x = _inproj0(x2, mod4, 0, lat_row, g0, w4, cw, cb, batch, seq, seq)
    ctx_rows = math.gcd(batch * ctx_len, seq)
    hy_c, zret_c = _inproj0(c2, mod4, 0, ctx_row, g0, w4, cw, cb, batch * ctx_len // ctx_rows, ctx_rows, ctx_len)

    filt = (hy_f_w1[0], hy_f_b1[0], hy_f_freq[0], hy_f_w2[0], hy_f_b2[0], hy_f_w3[0], hy_bias[0])
    yhyt_x = _hyena(hy_x, _filters(seq, *filt), batch, seq)
    yhyt_c = _hyena(hy_c, _filters(ctx_len, *filt), batch, ctx_len)
    yret_c, yret_x = _retention(zret_c, zret_x, ret_decay_f[0], ret_decay_b[0], ret_norm_g[0], batch, ctx_len, seq)

    wo_hy = er_out_w[0][:HY_W].astype(BF16)
    wo_ret = er_out_w[0][HY_W:].astype(BF16)
    perm = jnp.concatenate([jnp.arange(0, HEAD_DIM, 2), jnp.arange(1, HEAD_DIM, 2)])
    qk_cols = (jnp.arange(N_HEADS + N_KV)[:, None] * HEAD_DIM + perm[None, :]).reshape(-1)
    w1 = at_in_w[0]
    win_x = jnp.concatenate([w1[:, qk_cols], w1[:, ATT_W + KV_W:]], axis=1).astype(BF16)
    win_c = win_x[:, ATT_W:ATT_W + 2 * KV_W]
    qn = at_q_norm[0][perm].reshape(1, HEAD_DIM)
    kn = at_k_norm[0][perm].reshape(1, HEAD_DIM)
    g1 = norm_g[1].reshape(1, d)
    cosf, sinf = _rope_tables(seq)
    x1, q, kx, vx, gate = _mid(x2, yhyt_x.reshape(HY_W, batch * seq), yret_x, mod4, lat_row, wo_hy, wo_ret, g1,
                               win_x, qn, kn, cosf, sinf, batch, seq, tm, True)
    kc, vc = _mid(c2, yhyt_c.reshape(HY_W, batch * ctx_len), yret_c, mod4, ctx_row, wo_hy, wo_ret, g1,
                  win_c, qn, kn, None, None, batch, ctx_len, ctx_len, False)

    out = _attention(q, kc, vc, kx, vx, gate, x1, mod4, at_out_w[0].astype(BF16), final_norm_g.reshape(1, d),
                     batch, ctx_len, seq, min(512, seq))
    return out.reshape(batch, seq, d)
```

```python
import functools
import math

import jax
import jax.numpy as jnp
from jax import lax
from jax.experimental import pallas as pl
from jax.experimental.pallas import tpu as pltpu

F32 = jnp.float32
BF16 = jnp.bfloat16
HIGHEST = lax.Precision.HIGHEST

EPS = 1e-6
GRID_W = 64
HY_W = 512
HY_EMB = 33
HY_BANDS = (HY_EMB - 1) // 2
HY_ORDER = 64
HY_DECAY_TARGET = 1e-2
HY_FAST_PCT = 0.3
HY_SLOW_PCT = 1.5
RET_W = 512
RET_H = 4
RET_HD = RET_W // RET_H
CHUNK = 128
N_HEADS = 8
N_KV = 2
HEAD_DIM = 128
GROUP = N_HEADS // N_KV
ATT_W = N_HEADS * HEAD_DIM
KV_W = N_KV * HEAD_DIM
ROPE_THETA = 10000.0
Q_PRESCALE = HEAD_DIM ** -0.5 * math.log2(math.e)

MID_ROWS = 256
HY_CH = 256
LANES = 128
TOEP = 256
VMEM_LIMIT = 56 * 1024 * 1024
T_DEC, T_QDF, T_QDB, T_KDF, T_KDB, T_CDF, T_CDB = range(7)
N_TABLES = 7


def _params(sem):
    return pltpu.CompilerParams(dimension_semantics=sem, vmem_limit_bytes=VMEM_LIMIT)


def _silu(x):
    return x * (1.0 / (1.0 + jnp.exp(-x)))


def _dot(a, b):
    return jnp.dot(a, b, preferred_element_type=F32)


def _dot_nt(a, b):
    return lax.dot_general(a, b, (((1,), (1,)), ((), ())), preferred_element_type=F32)


def _dot_tn(a, b):
    return lax.dot_general(a, b, (((0,), (0,)), ((), ())), preferred_element_type=F32)


def _mod_norm(x, g, shift, scale):
    ms = jnp.mean(x * x, axis=-1, keepdims=True)
    return (x * lax.rsqrt(ms + EPS) * g) * (1.0 + scale) + shift


def _ada_kernel(cond_ref, w_ref, b_ref, o_ref):
    s = _silu(cond_ref[...])
    o_ref[...] = jnp.dot(s, w_ref[...], preferred_element_type=F32, precision=HIGHEST) + b_ref[...]


def _ada(cond, ada_w, ada_b):
    depth, d, n = ada_w.shape
    rows = cond.shape[0]
    tn = 1024
    return pl.pallas_call(
        _ada_kernel,
        grid=(depth, n // tn),
        in_specs=[
            pl.BlockSpec((rows, d), lambda i, j: (0, 0)),
            pl.BlockSpec((None, d, tn), lambda i, j: (i, 0, j)),
            pl.BlockSpec((None, 1, tn), lambda i, j: (i, 0, j)),
        ],
        out_specs=pl.BlockSpec((None, rows, tn), lambda i, j: (i, 0, j)),
        out_shape=jax.ShapeDtypeStruct((depth, rows, n), F32),
        compiler_params=_params(("arbitrary", "arbitrary")),
        name="ada",
    )(cond, ada_w, ada_b.reshape(depth, 1, n))


def _inproj_hy_kernel(x_ref, mod_ref, g_ref, w_ref, cw_ref, cb_ref, hy_ref, h_ref, *, period):
    seq, d = x_ref.shape
    rows = min(seq, 2 * MID_ROWS)
    for r in range(0, seq, rows):
        rs = slice(r, r + rows)
        h_ref[rs, :] = _mod_norm(x_ref[rs, :], g_ref[...], mod_ref[:, 0:d], mod_ref[:, d:2 * d]).astype(BF16)

    lane = lax.broadcasted_iota(jnp.int32, (1, seq), 1)
    first = lane % period == 0
    last = lane % period == period - 1

    def conv(a, part, c0):
        ch = slice(part * HY_W + c0, part * HY_W + c0 + a.shape[0])
        left = jnp.where(first, 0.0, pltpu.roll(a, 1, 1))
        right = jnp.where(last, 0.0, pltpu.roll(a, seq - 1, 1))
        return left * cw_ref[ch, 0:1] + a * cw_ref[ch, 1:2] + right * cw_ref[ch, 2:3] + cb_ref[ch, :]

    def hyena_part(p, combine):
        h = h_ref[...]
        sub = 8
        for c0 in range(0, HY_W, HY_CH):
            za = _dot_nt(w_ref[p, c0:c0 + HY_CH, :], h)
            zb = _dot_nt(w_ref[p, HY_W + c0:HY_W + c0 + HY_CH, :], h)
            for r in range(0, HY_CH, 2 * sub):
                pieces = [combine(za[r + i:r + i + sub, :], zb[r + i:r + i + sub, :], c0 + r + i) for i in (0, sub)]
                hy_ref[p, c0 + r:c0 + r + 2 * sub, :] = jnp.concatenate(pieces, axis=0).astype(BF16)

    hyena_part(0, lambda x0, gate, c0: conv(x0, 0, c0) * _silu(gate))
    hyena_part(1, lambda x1, v, c0: conv(x1, 1, c0) * conv(v, 2, c0))


def _inproj_hy(x2, mod4, layer, mod_row, g, w_hy, cw, cb, batch, seq, period):
    d = x2.shape[-1]
    return pl.pallas_call(
        functools.partial(_inproj_hy_kernel, period=period),
        grid=(batch,),
        in_specs=[
            pl.BlockSpec((seq, d), lambda b: (b, 0)),
            pl.BlockSpec((None, None, 1, 3 * d), lambda b: (layer, mod_row(b), 0, 0)),
            pl.BlockSpec((1, d), lambda b: (0, 0)),
            pl.BlockSpec(w_hy.shape, lambda b: (0, 0, 0), pipeline_mode=pl.Buffered(1)),
            pl.BlockSpec(cw.shape, lambda b: (0, 0), pipeline_mode=pl.Buffered(1)),
            pl.BlockSpec(cb.shape, lambda b: (0, 0), pipeline_mode=pl.Buffered(1)),
        ],
        out_specs=[
            pl.BlockSpec((2, HY_W, seq), lambda b: (0, 0, b)),
            pl.BlockSpec((seq, d), lambda b: (b, 0)),
        ],
        out_shape=[
            jax.ShapeDtypeStruct((2, HY_W, batch * seq), BF16),
            jax.ShapeDtypeStruct((batch * seq, d), BF16),
        ],
        compiler_params=_params(("arbitrary",)),
        name="inproj_hy",
    )(x2, mod4, g, w_hy, cw, cb)


def _inproj_ret_kernel(h_ref, w_ref, z_ref):
    n = w_ref.shape[1]
    step = 2 * RET_W
    for c in range(0, n, step):
        z_ref[:, c:c + step] = _dot(h_ref[...], w_ref[:, c:c + step]).astype(BF16)


def _inproj_ret(h, w_ret, tm):
    n, d = h.shape
    nr = w_ret.shape[1]
    return pl.pallas_call(
        _inproj_ret_kernel,
        grid=(n // tm,),
        in_specs=[pl.BlockSpec((tm, d), lambda i: (i, 0)), pl.BlockSpec((d, nr), lambda i: (0, 0))],
        out_specs=pl.BlockSpec((tm, nr), lambda i: (i, 0)),
        out_shape=jax.ShapeDtypeStruct((n, nr), BF16),
        compiler_params=_params(("arbitrary",)),
        name="inproj_ret",
    )(h, w_ret)


def _filter_kernel(w1t_ref, w1c_ref, w1s_ref, b1_ref, f_ref, w2_ref, b2_ref, w3_ref, bias_ref, o_ref, *, seq):
    two_l = o_ref.shape[-1]
    cb = o_ref.shape[0]
    lag = lax.broadcasted_iota(jnp.int32, (1, two_l), 1) - seq
    n = jnp.abs(lag).astype(F32)
    t = n / float(max(seq - 1, 1))
    band_i = lax.broadcasted_iota(jnp.int32, (HY_BANDS, 1), 0).astype(F32)
    bands = 1e-4 + band_i * ((HY_BANDS - 1 - 1e-4) / (HY_BANDS - 1))
    ang = (2.0 * math.pi / seq) * n * bands
    f = f_ref[...]
    pre = (w1t_ref[...] * t
           + jnp.dot(w1c_ref[...], jnp.cos(ang), preferred_element_type=F32, precision=HIGHEST)
           + jnp.dot(w1s_ref[...], -jnp.sin(ang), preferred_element_type=F32, precision=HIGHEST))
    h = jnp.sin(f * (pre + b1_ref[...]))
    h = jnp.sin(f * (jnp.dot(w2_ref[...], h, preferred_element_type=F32, precision=HIGHEST) + b2_ref[...]))
    h_f = jnp.dot(w3_ref[0], h, preferred_element_type=F32, precision=HIGHEST)
    h_b = jnp.dot(w3_ref[1], h, preferred_element_type=F32, precision=HIGHEST)
    max_decay = math.log(HY_DECAY_TARGET) / HY_FAST_PCT
    min_decay = math.log(HY_DECAY_TARGET) / HY_SLOW_PCT
    ch = (lax.broadcasted_iota(jnp.int32, (cb, 1), 0) + pl.program_id(0) * cb).astype(F32)
    deltas = jnp.abs(min_decay + ch * ((max_decay - min_decay) / (HY_W - 1)))
    window = jnp.exp(-t * deltas)
    k = jnp.where(lag >= 0, h_f, h_b) * window
    k = jnp.where(lag == -seq, 0.0, k)
    o_ref[...] = k + jnp.where(lag == 0, bias_ref[...], 0.0)


def _filters(seq, w1, b1, freq, w2, b2, w3, bias):
    cb = 128
    w1t = w1.T
    args = (
        w1t[:, 0:1], w1t[:, 1:1 + HY_BANDS], w1t[:, 1 + HY_BANDS:],
        b1.reshape(HY_ORDER, 1), freq.reshape(HY_ORDER, 1), w2.T, b2.reshape(HY_ORDER, 1),
        w3.T.reshape(2, HY_W, HY_ORDER), bias.reshape(HY_W, 1),
    )
    full = lambda a: pl.BlockSpec(a.shape, lambda i: (0,) * a.ndim)
    in_specs = [full(a) for a in args[:7]] + [
        pl.BlockSpec((2, cb, HY_ORDER), lambda i: (0, i, 0)),
        pl.BlockSpec((cb, 1), lambda i: (i, 0)),
    ]
    return pl.pallas_call(
        functools.partial(_filter_kernel, seq=seq),
        grid=(HY_W // cb,),
        in_specs=in_specs,
        out_specs=pl.BlockSpec((cb, 2 * seq), lambda i: (i, 0)),
        out_shape=jax.ShapeDtypeStruct((HY_W, 2 * seq), F32),
        compiler_params=_params(("arbitrary",)),
        name="hyena_filters",
    )(*args)


def _hyena_kernel(xg_ref, u_ref, kext_ref, out_ref, t_all, u2_all, y2_all, *, seq, batch):
    cblk = out_ref.shape[0]
    nb = seq // TOEP
    half = seq // LANES
    row_i = lax.broadcasted_iota(jnp.int32, (LANES, LANES), 0)
    col_i = lax.broadcasted_iota(jnp.int32, (LANES, LANES), 1)
    upper = col_i >= row_i

    def channel(c, t_ref, u2_ref, y2_ref):
        krow = kext_ref[c]

        def rotated(q):
            a = jnp.broadcast_to(krow[:, q * LANES:(q + 1) * LANES], (LANES, LANES))
            return pltpu.roll(a, 0, 1, stride=1, stride_axis=0)

        e_lo = -(2 * nb - 1)
        r_prev = rotated(half + e_lo - 1)
        for e in range(e_lo, 2 * nb):
            r_cur = rotated(half + e)
            g = jnp.where(upper, r_cur, r_prev).astype(BF16)
            r_prev = r_cur
            if e % 2 == 0:
                d = e // 2
                t_ref[d + nb - 1, 0:LANES, 0:LANES] = g
                t_ref[d + nb - 1, LANES:TOEP, LANES:TOEP] = g
            else:
                d = (e - 1) // 2
                if abs(d) <= nb - 1:
                    t_ref[d + nb - 1, 0:LANES, LANES:TOEP] = g
                d = (e + 1) // 2
                if abs(d) <= nb - 1:
                    t_ref[d + nb - 1, LANES:TOEP, 0:LANES] = g
        for j in range(nb):
            u2_ref[j * batch:(j + 1) * batch, :] = u_ref[c, :, j * TOEP:(j + 1) * TOEP]
        y2_ref[...] = _dot(u2_ref[...], t_ref[nb - 1])
        for d in list(range(1, nb)) + list(range(-(nb - 1), 0)):
            j_lo, j_hi = max(0, -d), nb - max(0, d)
            res = _dot(u2_ref[j_lo * batch:j_hi * batch, :], t_ref[d + nb - 1])
            y2_ref[(j_lo + d) * batch:(j_hi + d) * batch, :] += res
        for j in range(nb):
            sl = slice(j * TOEP, (j + 1) * TOEP)
            out_ref[c, :, sl] = (xg_ref[c, :, sl].astype(F32) * y2_ref[j * batch:(j + 1) * batch, :]).astype(BF16)

    for c in range(cblk):
        channel(c, t_all.at[c], u2_all.at[c], y2_all.at[c])


def _hyena(hy, kext, batch, seq):
    cblk = 8
    nb = seq // TOEP
    hy4 = hy.reshape(2, HY_W, batch, seq)
    part = lambda p: pl.BlockSpec((None, cblk, batch, seq), lambda i: (p, i, 0, 0))
    return pl.pallas_call(
        functools.partial(_hyena_kernel, seq=seq, batch=batch),
        grid=(HY_W // cblk,),
        in_specs=[part(0), part(1), pl.BlockSpec((cblk, 1, 2 * seq), lambda i: (i, 0, 0))],
        out_specs=pl.BlockSpec((cblk, batch, seq), lambda i: (i, 0, 0)),
        out_shape=jax.ShapeDtypeStruct((HY_W, batch, seq), BF16),
        scratch_shapes=[
            pltpu.VMEM((cblk, 2 * nb - 1, TOEP, TOEP), BF16),
            pltpu.VMEM((cblk, nb * batch, TOEP), BF16),
            pltpu.VMEM((cblk, nb * batch, TOEP), F32),
        ],
        compiler_params=_params(("arbitrary",)),
        name="hyena",
    )(hy4, hy4, kext.reshape(HY_W, 1, 2 * seq))


def _retention_tables_kernel(af_ref, ab_ref, tab_ref):
    k_scale = RET_HD ** -0.5
    ri = lax.broadcasted_iota(jnp.int32, (CHUNK, CHUNK), 0).astype(F32)
    ci = lax.broadcasted_iota(jnp.int32, (CHUNK, CHUNK), 1).astype(F32)
    diff = ri - ci
    for h in range(RET_H):
        log_f = -jnp.exp(jnp.full((CHUNK, CHUNK), af_ref[h], F32))
        log_b = -jnp.exp(jnp.full((CHUNK, CHUNK), ab_ref[h], F32))
        dec_f = jnp.where(diff >= 0, jnp.exp(log_f * jnp.maximum(diff, 0.0)), 0.0)
        dec_b = jnp.where(diff <= 0, jnp.exp(log_b * jnp.maximum(-diff, 0.0)), 0.0)
        tab_ref[h, T_DEC] = (dec_f + dec_b) * k_scale
        tab_ref[h, T_QDF] = jnp.exp(log_f * (ri + 1.0))
        tab_ref[h, T_QDB] = jnp.exp(log_b * (CHUNK - ri))
        tab_ref[h, T_KDF] = jnp.exp(log_f * (CHUNK - 1.0 - ri)) * k_scale
        tab_ref[h, T_KDB] = jnp.exp(log_b * ri) * k_scale
        tab_ref[h, T_CDF] = jnp.exp(log_f * CHUNK)
        tab_ref[h, T_CDB] = jnp.exp(log_b * CHUNK)


def _retention_tables(a_f, a_b):
    smem = pl.BlockSpec(memory_space=pltpu.SMEM)
    return pl.pallas_call(
        _retention_tables_kernel,
        in_specs=[smem, smem],
        out_shape=jax.ShapeDtypeStruct((RET_H, N_TABLES, CHUNK, CHUNK), F32),
        name="retention_tables",
    )(a_f, a_b)


def _retention_kernel(tab_ref, zc_ref, zx_ref, ng_ref, oc_ref, ox_ref, sf_ref, sb_ref, sbc_ref, sbx_ref):
    hd = RET_HD
    sf_ref[...] = jnp.zeros_like(sf_ref)
    sb_ref[...] = jnp.zeros_like(sb_ref)

    def head_cols(part, h):
        return slice((part * RET_H + h) * hd, (part * RET_H + h + 1) * hd)

    def state_update(s, k, v, kd, cd):
        kdt = jnp.transpose(k.astype(F32) * kd).astype(BF16)
        return s * cd + _dot(kdt, v)

    def state_pass(z_ref, store_ref):
        nchunk = z_ref.shape[0] // CHUNK

        for i in range(nchunk):
            for t, s_ref, lo, t_kd, t_cd in ((i, sf_ref, 0, T_KDF, T_CDF),
                                             (nchunk - 1 - i, sb_ref, hd, T_KDB, T_CDB)):
                rows = slice(t * CHUNK, (t + 1) * CHUNK)
                for h in range(RET_H):
                    k = z_ref[rows, head_cols(1, h)]
                    v = z_ref[rows, head_cols(2, h)]
                    s = s_ref[h]
                    store_ref[t, h, :, lo:lo + hd] = s.astype(BF16)
                    s_ref[h] = state_update(s, k, v, tab_ref[h, t_kd], tab_ref[h, t_cd])

    def out_pass(z_ref, store_ref, out_ref):
        nchunk = z_ref.shape[0] // CHUNK

        for t in range(nchunk):
            rows = slice(t * CHUNK, (t + 1) * CHUNK)
            for h in range(RET_H):
                q = z_ref[rows, head_cols(0, h)]
                k = z_ref[rows, head_cols(1, h)]
                v = z_ref[rows, head_cols(2, h)]
                scores = _dot_nt(q, k) * tab_ref[h, T_DEC]
                cross = _dot(q, store_ref[t, h])
                o = (_dot(scores.astype(BF16), v) + cross[:, :hd] * tab_ref[h, T_QDF]
                     + cross[:, hd:] * tab_ref[h, T_QDB])
                mu = jnp.mean(o, axis=-1, keepdims=True)
                oc = o - mu
                var = jnp.mean(oc * oc, axis=-1, keepdims=True)
                y = oc * lax.rsqrt(var + EPS) * ng_ref[:, h * hd:(h + 1) * hd]
                gate = _silu(z_ref[rows, head_cols(3, h)].astype(F32))
                out_ref[rows, h * hd:(h + 1) * hd] = (y * gate).astype(BF16)

    state_pass(zc_ref, sbc_ref)
    state_pass(zx_ref, sbx_ref)
    out_pass(zc_ref, sbc_ref, oc_ref)
    out_pass(zx_ref, sbx_ref, ox_ref)


def _retention(zret_c, zret_x, a_f, a_b, norm_g, batch, ctx_len, seq):
    hd = RET_HD
    tables = _retention_tables(a_f, a_b)
    return pl.pallas_call(
        _retention_kernel,
        grid=(batch,),
        in_specs=[pl.BlockSpec(tables.shape, lambda b: (0, 0, 0, 0)),
                  pl.BlockSpec((ctx_len, 4 * RET_W), lambda b: (b, 0)),
                  pl.BlockSpec((seq, 4 * RET_W), lambda b: (b, 0)),
                  pl.BlockSpec((1, RET_W), lambda b: (0, 0))],
        out_specs=[pl.BlockSpec((ctx_len, RET_W), lambda b: (b, 0)),
                   pl.BlockSpec((seq, RET_W), lambda b: (b, 0))],
        out_shape=[jax.ShapeDtypeStruct((batch * ctx_len, RET_W), BF16),
                   jax.ShapeDtypeStruct((batch * seq, RET_W), BF16)],
        scratch_shapes=[
            pltpu.VMEM((RET_H, hd, hd), F32),
            pltpu.VMEM((RET_H, hd, hd), F32),
            pltpu.VMEM((ctx_len // CHUNK, RET_H, hd, 2 * hd), BF16),
            pltpu.VMEM((seq // CHUNK, RET_H, hd, 2 * hd), BF16),
        ],
        compiler_params=_params(("arbitrary",)),
        name="retention",
    )(tables, zret_c, zret_x, norm_g.reshape(1, RET_W))


def _head_norm_rope(z, gain, cos, sin):
    ms = jnp.mean(z * z, axis=-1, keepdims=True)
    y = z * lax.rsqrt(ms + EPS) * gain
    if cos is None:
        return y
    return y * cos + pltpu.roll(y, HEAD_DIM // 2, 1) * sin


def _store_v_aug(v_ref, v):
    ones = jnp.ones((v.shape[0], HEAD_DIM), BF16)
    for hd in range(N_KV):
        v_ref[:, 2 * hd * HEAD_DIM:(2 * hd + 1) * HEAD_DIM] = v[:, hd * HEAD_DIM:(hd + 1) * HEAD_DIM].astype(BF16)
        v_ref[:, (2 * hd + 1) * HEAD_DIM:(2 * hd + 2) * HEAD_DIM] = ones


def _mid_kernel(x_ref, yhyt_ref, yret_ref, mod0_ref, mod1_ref, wo_hy_ref, wo_ret_ref, g1_ref, win_ref,
                qn_ref, kn_ref, *rest, latent):
    d = x_ref.shape[-1]
    if latent:
        cos_ref, sin_ref, xo_ref, q_ref, k_ref, v_ref, g_ref = rest
    else:
        k_ref, v_ref = rest
    tm = x_ref.shape[0]
    sub = min(tm, MID_ROWS)
    for r in range(0, tm, sub):
        rs = slice(r, r + sub)
        cos, sin = (cos_ref[rs, :], sin_ref[rs, :]) if latent else (None, None)
        y = _dot_tn(yhyt_ref[:, rs], wo_hy_ref[...]) + _dot(yret_ref[rs, :], wo_ret_ref[...])
        x = x_ref[rs, :] + mod0_ref[:, 2 * d:3 * d] * y
        if latent:
            xo_ref[rs, :] = x
        h = _mod_norm(x, g1_ref[...], mod1_ref[:, 0:d], mod1_ref[:, d:2 * d]).astype(BF16)
        off = 0
        pair = 2 * HEAD_DIM
        if latent:
            for hp in range(N_HEADS // 2):
                z2 = _dot(h, win_ref[:, off:off + pair])
                for i in range(2):
                    q = _head_norm_rope(z2[:, i * HEAD_DIM:(i + 1) * HEAD_DIM], qn_ref[...], cos, sin) * Q_PRESCALE
                    q_ref[rs, off + i * HEAD_DIM:off + (i + 1) * HEAD_DIM] = q.astype(BF16)
                off += pair
        z2 = _dot(h, win_ref[:, off:off + pair])
        for i in range(N_KV):
            k = _head_norm_rope(z2[:, i * HEAD_DIM:(i + 1) * HEAD_DIM], kn_ref[...], cos, sin)
            k_ref[rs, i * HEAD_DIM:(i + 1) * HEAD_DIM] = k.astype(BF16)
        off += pair
        _store_v_aug(v_ref.at[rs, :], _dot(h, win_ref[:, off:off + KV_W]))
        off += KV_W
        if latent:
            g_ref[rs, :] = _dot(h, win_ref[:, off:off + ATT_W]).astype(BF16)


def _mid(x2, yhyt, yret, mod4, mod_row, wo_hy, wo_ret, g1, win, qn, kn, cosf, sinf, batch, seq, tm, latent):
    d = x2.shape[-1]
    nt = seq // tm
    tok = lambda w: pl.BlockSpec((tm, w), lambda b, t: (b * nt + t, 0))
    const = lambda a: pl.BlockSpec(a.shape, lambda b, t: (0,) * a.ndim)
    modspec = lambda layer: pl.BlockSpec((None, None, 1, 3 * d), lambda b, t: (layer, mod_row(b), 0, 0))
    in_specs = [tok(d), pl.BlockSpec((HY_W, tm), lambda b, t: (0, b * nt + t)), tok(RET_W),
                modspec(0), modspec(1), const(wo_hy), const(wo_ret), const(g1), const(win), const(qn), const(kn)]
    args = [x2, yhyt, yret, mod4, mod4, wo_hy, wo_ret, g1, win, qn, kn]
    n = batch * seq
    if latent:
        in_specs += [pl.BlockSpec((tm, HEAD_DIM), lambda b, t: (t, 0))] * 2
        args += [cosf, sinf]
        out_specs = [tok(d), tok(ATT_W), tok(KV_W), tok(2 * KV_W), tok(ATT_W)]
        out_shape = [jax.ShapeDtypeStruct((n, d), F32), jax.ShapeDtypeStruct((n, ATT_W), BF16),
                     jax.ShapeDtypeStruct((n, KV_W), BF16), jax.ShapeDtypeStruct((n, 2 * KV_W), BF16),
                     jax.ShapeDtypeStruct((n, ATT_W), BF16)]
    else:
        out_specs = [tok(KV_W), tok(2 * KV_W)]
        out_shape = [jax.ShapeDtypeStruct((n, KV_W), BF16), jax.ShapeDtypeStruct((n, 2 * KV_W), BF16)]
    return pl.pallas_call(
        functools.partial(_mid_kernel, latent=latent),
        grid=(batch, nt),
        in_specs=in_specs,
        out_specs=out_specs,
        out_shape=out_shape,
        compiler_params=_params(("arbitrary", "arbitrary")),
        name="mid_latent" if latent else "mid_ctx",
    )(*args)


def _attn_kernel(q_ref, kc_ref, vc_ref, kx_ref, vx_ref, g_ref, x_ref, mod_ref, w_ref, fg_ref, out_ref):
    d = x_ref.shape[-1]
    heads = []
    for hd in range(N_HEADS):
        kvh = hd // GROUP
        ksl = slice(kvh * HEAD_DIM, (kvh + 1) * HEAD_DIM)
        vsl = slice(2 * kvh * HEAD_DIM, 2 * (kvh + 1) * HEAD_DIM)
        sl = slice(hd * HEAD_DIM, (hd + 1) * HEAD_DIM)
        q = q_ref[:, sl]
        s_c = _dot_nt(q, kc_ref[:, ksl])
        s_x = _dot_nt(q, kx_ref[:, ksl])
        m = jnp.maximum(jnp.max(s_c, axis=-1, keepdims=True), jnp.max(s_x, axis=-1, keepdims=True))
        p_c = jnp.exp2(s_c - m).astype(BF16)
        p_x = jnp.exp2(s_x - m).astype(BF16)
        o = _dot(p_c, vc_ref[:, vsl]) + _dot(p_x, vx_ref[:, vsl])
        gate = _silu(g_ref[:, sl].astype(F32))
        heads.append((gate * (o[:, :HEAD_DIM] / o[:, HEAD_DIM:])).astype(BF16))
    y = _dot(jnp.concatenate(heads, axis=1), w_ref[...])
    x = x_ref[...] + mod_ref[:, 2 * d:3 * d] * y
    ms = jnp.mean(x * x, axis=-1, keepdims=True)
    out_ref[...] = x * lax.rsqrt(ms + EPS) * fg_ref[...]


def _attention(q, kc, vc, kx, vx, g, x2, mod4, w, fg, batch, ctx_len, seq, tq):
    d = x2.shape[-1]
    nt = seq // tq
    kv = lambda rows, w_: pl.BlockSpec((rows, w_), lambda b, t: (b, 0))
    tok = lambda w_: pl.BlockSpec((tq, w_), lambda b, t: (b * nt + t, 0))
    const = lambda a: pl.BlockSpec(a.shape, lambda b, t: (0,) * a.ndim)
    return pl.pallas_call(
        _attn_kernel,
        grid=(batch, nt),
        in_specs=[tok(ATT_W), kv(ctx_len, KV_W), kv(ctx_len, 2 * KV_W), kv(seq, KV_W), kv(seq, 2 * KV_W),
                  tok(ATT_W), tok(d),
                  pl.BlockSpec((None, None, 1, 3 * d), lambda b, t: (1, b, 0, 0)), const(w), const(fg)],
        out_specs=tok(d),
        out_shape=jax.ShapeDtypeStruct((batch * seq, d), F32),
        compiler_params=_params(("arbitrary", "arbitrary")),
        name="attention",
    )(q, kc, vc, kx, vx, g, x2, mod4, w, fg)


def _rope_tables(seq):
    rows = seq // GRID_W
    r = jnp.repeat(jnp.arange(rows, dtype=F32), GRID_W)
    col = jnp.tile(jnp.arange(GRID_W, dtype=F32), rows)
    half = HEAD_DIM // 2
    inv = ROPE_THETA ** (-jnp.arange(0, half, 2, dtype=F32) / half)
    ang = jnp.concatenate([r[:, None] * inv, col[:, None] * inv], axis=-1)
    cos, sin = jnp.cos(ang), jnp.sin(ang)
    return jnp.concatenate([cos, cos], axis=-1), jnp.concatenate([-sin, sin], axis=-1)


def kernel(x, c, ctx, c_ctx, norm_g, ada_w, ada_b, er_in_w, er_out_w, hy_conv_w, hy_conv_b, hy_f_w1, hy_f_b1, hy_f_freq, hy_f_w2, hy_f_b2, hy_f_w3, hy_bias, ret_decay_f, ret_decay_b, ret_norm_g, at_in_w, at_out_w, at_q_norm, at_k_norm, final_norm_g):
    batch, seq, d = x.shape
    ctx_len = ctx.shape[1]
    assert seq % TOEP == 0 and ctx_len % TOEP == 0 and seq % GRID_W == 0 and d == 2 * HY_W
    tm = min(512, seq)

    rows = -(-(batch + 1) // 8) * 8
    cond = jnp.concatenate([c, c_ctx[None, :], jnp.zeros((rows - batch - 1, d), F32)], axis=0)
    mod4 = _ada(cond, ada_w, ada_b).reshape(ada_w.shape[0], rows, 1, 3 * d)
    lat_row = lambda b: b
    ctx_row = lambda b: batch

    x2 = x.reshape(batch * seq, d)
    c2 = ctx.reshape(batch * ctx_len, d)

    in_w = er_in_w[0].astype(BF16)
    col = lambda p: in_w[:, p * HY_W:(p + 1) * HY_W].T
    w_hy = jnp.stack([jnp.concatenate([col(0), col(3)], axis=0), jnp.concatenate([col(1), col(2)], axis=0)])
    w_ret = in_w[:, 4 * HY_W:]
    cw = hy_conv_w[0].T
    cb = hy_conv_b[0].reshape(3 * HY_W, 1)
    g0 = norm_g[0].reshape(1, d)
    hy_x, h_x = _inproj_hy(x2, mod4, 0, lat_row, g0, w_hy, cw, cb, batch, seq, seq)
    ctx_rows = math.gcd(batch * ctx_len, seq)
    hy_c, h_c = _inproj_hy(c2, mod4, 0, ctx_row, g0, w_hy, cw, cb, batch * ctx_len // ctx_rows, ctx_rows, ctx_len)
    zret_x = _inproj_ret(h_x, w_ret, tm)
    zret_c = _inproj_ret(h_c, w_ret, tm)

    filt = (hy_f_w1[0], hy_f_b1[0], hy_f_freq[0], hy_f_w2[0], hy_f_b2[0], hy_f_w3[0], hy_bias[0])
    yhyt_x = _hyena(hy_x, _filters(seq, *filt), batch, seq)
    yhyt_c = _hyena(hy_c, _filters(ctx_len, *filt), batch, ctx_len)
    yret_c, yret_x = _retention(zret_c, zret_x, ret_decay_f[0], ret_decay_b[0], ret_norm_g[0], batch, ctx_len, seq)

    wo_hy = er_out_w[0][:HY_W].astype(BF16)
    wo_ret = er_out_w[0][HY_W:].astype(BF16)
    perm = jnp.concatenate([jnp.arange(0, HEAD_DIM, 2), jnp.arange(1, HEAD_DIM, 2)])
    qk_cols = (jnp.arange(N_HEADS + N_KV)[:, None] * HEAD_DIM + perm[None, :]).reshape(-1)
    w1 = at_in_w[0]
    win_x = jnp.concatenate([w1[:, qk_cols], w1[:, ATT_W + KV_W:]], axis=1).astype(BF16)
    win_c = win_x[:, ATT_W:ATT_W + 2 * KV_W]
    qn = at_q_norm[0][perm].reshape(1, HEAD_DIM)
    kn = at_k_norm[0][perm].reshape(1, HEAD_DIM)
    g1 = norm_g[1].reshape(1, d)
    cosf, sinf = _rope_tables(seq)
    x1, q, kx, vx, gate = _mid(x2, yhyt_x.reshape(HY_W, batch * seq), yret_x, mod4, lat_row, wo_hy, wo_ret, g1,
                               win_x, qn, kn, cosf, sinf, batch, seq, tm, True)
    kc, vc = _mid(c2, yhyt_c.reshape(HY_W, batch * ctx_len), yret_c, mod4, ctx_row, wo_hy, wo_ret, g1,
                  win_c, qn, kn, None, None, batch, ctx_len, ctx_len, False)

    out = _attention(q, kc, vc, kx, vx, gate, x1, mod4, at_out_w[0].astype(BF16), final_norm_g.reshape(1, d),
                     batch, ctx_len, seq, min(512, seq))
    return out.reshape(batch, seq, d)
```

```python
import functools
import math

import jax
import jax.numpy as jnp
from jax import lax
from jax.experimental import pallas as pl
from jax.experimental.pallas import tpu as pltpu

F32 = jnp.float32
BF16 = jnp.bfloat16
HIGHEST = lax.Precision.HIGHEST

EPS = 1e-6
GRID_W = 64
HY_W = 512
HY_EMB = 33
HY_BANDS = (HY_EMB - 1) // 2
HY_ORDER = 64
HY_DECAY_TARGET = 1e-2
HY_FAST_PCT = 0.3
HY_SLOW_PCT = 1.5
RET_W = 512
RET_H = 4
RET_HD = RET_W // RET_H
CHUNK = 128
N_HEADS = 8
N_KV = 2
HEAD_DIM = 128
GROUP = N_HEADS // N_KV
ATT_W = N_HEADS * HEAD_DIM
KV_W = N_KV * HEAD_DIM
ROPE_THETA = 10000.0
Q_PRESCALE = HEAD_DIM ** -0.5 * math.log2(math.e)

MID_ROWS = 256
HY_CH = 256
LANES = 128
TOEP = 256
VMEM_LIMIT = 56 * 1024 * 1024
T_DEC, T_QDF, T_QDB, T_KDF, T_KDB, T_CDF, T_CDB = range(7)
N_TABLES = 7


def _params(sem):
    return pltpu.CompilerParams(dimension_semantics=sem, vmem_limit_bytes=VMEM_LIMIT)


def _silu(x):
    return x * (1.0 / (1.0 + jnp.exp(-x)))


def _dot(a, b):
    return jnp.dot(a, b, preferred_element_type=F32)


def _dot_nt(a, b):
    return lax.dot_general(a, b, (((1,), (1,)), ((), ())), preferred_element_type=F32)


def _dot_tn(a, b):
    return lax.dot_general(a, b, (((0,), (0,)), ((), ())), preferred_element_type=F32)


def _mod_norm(x, g, shift, scale):
    ms = jnp.mean(x * x, axis=-1, keepdims=True)
    return (x * lax.rsqrt(ms + EPS) * g) * (1.0 + scale) + shift


def _ada_kernel(cond_ref, w_ref, b_ref, o_ref):
    s = _silu(cond_ref[...])
    o_ref[...] = jnp.dot(s, w_ref[...], preferred_element_type=F32, precision=HIGHEST) + b_ref[...]


def _ada(cond, ada_w, ada_b):
    depth, d, n = ada_w.shape
    rows = cond.shape[0]
    tn = 1024
    return pl.pallas_call(
        _ada_kernel,
        grid=(depth, n // tn),
        in_specs=[
            pl.BlockSpec((rows, d), lambda i, j: (0, 0)),
            pl.BlockSpec((None, d, tn), lambda i, j: (i, 0, j)),
            pl.BlockSpec((None, 1, tn), lambda i, j: (i, 0, j)),
        ],
        out_specs=pl.BlockSpec((None, rows, tn), lambda i, j: (i, 0, j)),
        out_shape=jax.ShapeDtypeStruct((depth, rows, n), F32),
        compiler_params=_params(("arbitrary", "arbitrary")),
        name="ada",
    )(cond, ada_w, ada_b.reshape(depth, 1, n))


def _inproj_hy_kernel(x_ref, mod_ref, g_ref, w_ref, cw_ref, cb_ref, hy_ref, h_ref, *, period):
    seq, d = x_ref.shape
    rows = min(seq, 2 * MID_ROWS)
    for r in range(0, seq, rows):
        rs = slice(r, r + rows)
        h_ref[rs, :] = _mod_norm(x_ref[rs, :], g_ref[...], mod_ref[:, 0:d], mod_ref[:, d:2 * d]).astype(BF16)

    lane = lax.broadcasted_iota(jnp.int32, (1, seq), 1)
    first = lane % period == 0
    last = lane % period == period - 1

    def conv(a, part, c0):
        ch = slice(part * HY_W + c0, part * HY_W + c0 + a.shape[0])
        left = jnp.where(first, 0.0, pltpu.roll(a, 1, 1))
        right = jnp.where(last, 0.0, pltpu.roll(a, seq - 1, 1))
        return left * cw_ref[ch, 0:1] + a * cw_ref[ch, 1:2] + right * cw_ref[ch, 2:3] + cb_ref[ch, :]

    def hyena_part(p, combine):
        h = h_ref[...]
        sub = 8
        for c0 in range(0, HY_W, HY_CH):
            za = _dot_nt(w_ref[p, c0:c0 + HY_CH, :], h)
            zb = _dot_nt(w_ref[p, HY_W + c0:HY_W + c0 + HY_CH, :], h)
            for r in range(0, HY_CH, 2 * sub):
                pieces = [combine(za[r + i:r + i + sub, :], zb[r + i:r + i + sub, :], c0 + r + i) for i in (0, sub)]
                hy_ref[p, c0 + r:c0 + r + 2 * sub, :] = jnp.concatenate(pieces, axis=0).astype(BF16)

    hyena_part(0, lambda x0, gate, c0: conv(x0, 0, c0) * _silu(gate))
    hyena_part(1, lambda x1, v, c0: conv(x1, 1, c0) * conv(v, 2, c0))


def _inproj_hy(x2, mod4, layer, mod_row, g, w_hy, cw, cb, batch, seq, period):
    d = x2.shape[-1]
    return pl.pallas_call(
        functools.partial(_inproj_hy_kernel, period=period),
        grid=(batch,),
        in_specs=[
            pl.BlockSpec((seq, d), lambda b: (b, 0)),
            pl.BlockSpec((None, None, 1, 3 * d), lambda b: (layer, mod_row(b), 0, 0)),
            pl.BlockSpec((1, d), lambda b: (0, 0)),
            pl.BlockSpec(w_hy.shape, lambda b: (0, 0, 0), pipeline_mode=pl.Buffered(1)),
            pl.BlockSpec(cw.shape, lambda b: (0, 0), pipeline_mode=pl.Buffered(1)),
            pl.BlockSpec(cb.shape, lambda b: (0, 0), pipeline_mode=pl.Buffered(1)),
        ],
        out_specs=[
            pl.BlockSpec((2, HY_W, seq), lambda b: (0, 0, b)),
            pl.BlockSpec((seq, d), lambda b: (b, 0)),
        ],
        out_shape=[
            jax.ShapeDtypeStruct((2, HY_W, batch * seq), BF16),
            jax.ShapeDtypeStruct((batch * seq, d), BF16),
        ],
        compiler_params=_params(("arbitrary",)),
        name="inproj_hy",
    )(x2, mod4, g, w_hy, cw, cb)


def _inproj_ret_kernel(h_ref, w_ref, z_ref):
    n = w_ref.shape[1]
    step = 2 * RET_W
    for c in range(0, n, step):
        z_ref[:, c:c + step] = _dot(h_ref[...], w_ref[:, c:c + step]).astype(BF16)


def _inproj_ret(h, w_ret, tm):
    n, d = h.shape
    nr = w_ret.shape[1]
    return pl.pallas_call(
        _inproj_ret_kernel,
        grid=(n // tm,),
        in_specs=[pl.BlockSpec((tm, d), lambda i: (i, 0)), pl.BlockSpec((d, nr), lambda i: (0, 0))],
        out_specs=pl.BlockSpec((tm, nr), lambda i: (i, 0)),
        out_shape=jax.ShapeDtypeStruct((n, nr), BF16),
        compiler_params=_params(("arbitrary",)),
        name="inproj_ret",
    )(h, w_ret)


def _filter_kernel(w1t_ref, w1c_ref, w1s_ref, b1_ref, f_ref, w2_ref, b2_ref, w3_ref, bias_ref, o_ref, *, seq):
    two_l = o_ref.shape[-1]
    cb = o_ref.shape[0]
    lag = lax.broadcasted_iota(jnp.int32, (1, two_l), 1) - seq
    n = jnp.abs(lag).astype(F32)
    t = n / float(max(seq - 1, 1))
    band_i = lax.broadcasted_iota(jnp.int32, (HY_BANDS, 1), 0).astype(F32)
    bands = 1e-4 + band_i * ((HY_BANDS - 1 - 1e-4) / (HY_BANDS - 1))
    ang = (2.0 * math.pi / seq) * n * bands
    f = f_ref[...]
    pre = (w1t_ref[...] * t
           + jnp.dot(w1c_ref[...], jnp.cos(ang), preferred_element_type=F32, precision=HIGHEST)
           + jnp.dot(w1s_ref[...], -jnp.sin(ang), preferred_element_type=F32, precision=HIGHEST))
    h = jnp.sin(f * (pre + b1_ref[...]))
    h = jnp.sin(f * (jnp.dot(w2_ref[...], h, preferred_element_type=F32, precision=HIGHEST) + b2_ref[...]))
    h_b = jnp.dot(w3_ref[1], h[:, :seq], preferred_element_type=F32, precision=HIGHEST)
    h_f = jnp.dot(w3_ref[0], h[:, seq:], preferred_element_type=F32, precision=HIGHEST)
    max_decay = math.log(HY_DECAY_TARGET) / HY_FAST_PCT
    min_decay = math.log(HY_DECAY_TARGET) / HY_SLOW_PCT
    ch = (lax.broadcasted_iota(jnp.int32, (cb, 1), 0) + pl.program_id(0) * cb).astype(F32)
    deltas = jnp.abs(min_decay + ch * ((max_decay - min_decay) / (HY_W - 1)))
    window = jnp.exp(-t * deltas)
    k = jnp.concatenate([h_b, h_f], axis=1) * window
    k = jnp.where(lag == -seq, 0.0, k)
    o_ref[...] = k + jnp.where(lag == 0, bias_ref[...], 0.0)


def _filters(seq, w1, b1, freq, w2, b2, w3, bias):
    cb = 128
    w1t = w1.T
    args = (
        w1t[:, 0:1], w1t[:, 1:1 + HY_BANDS], w1t[:, 1 + HY_BANDS:],
        b1.reshape(HY_ORDER, 1), freq.reshape(HY_ORDER, 1), w2.T, b2.reshape(HY_ORDER, 1),
        w3.T.reshape(2, HY_W, HY_ORDER), bias.reshape(HY_W, 1),
    )
    full = lambda a: pl.BlockSpec(a.shape, lambda i: (0,) * a.ndim)
    in_specs = [full(a) for a in args[:7]] + [
        pl.BlockSpec((2, cb, HY_ORDER), lambda i: (0, i, 0)),
        pl.BlockSpec((cb, 1), lambda i: (i, 0)),
    ]
    return pl.pallas_call(
        functools.partial(_filter_kernel, seq=seq),
        grid=(HY_W // cb,),
        in_specs=in_specs,
        out_specs=pl.BlockSpec((cb, 2 * seq), lambda i: (i, 0)),
        out_shape=jax.ShapeDtypeStruct((HY_W, 2 * seq), F32),
        compiler_params=_params(("arbitrary",)),
        name="hyena_filters",
    )(*args)


def _hyena_kernel(*refs, seqs, batch):
    ns = len(seqs)
    ins, outs, scr = refs[:3 * ns], refs[3 * ns:4 * ns], refs[4 * ns:]
    cblk = outs[0].shape[0]
    row_i = lax.broadcasted_iota(jnp.int32, (LANES, LANES), 0)
    col_i = lax.broadcasted_iota(jnp.int32, (LANES, LANES), 1)
    upper = col_i >= row_i

    def channel(c, seq, xg_ref, u_ref, kext_ref, out_ref, t_ref, u2_ref, y2_ref):
        nb = seq // TOEP
        half = seq // LANES
        krow = kext_ref[c:c + 1, :]

        def rotated(q):
            a = jnp.broadcast_to(krow[:, q * LANES:(q + 1) * LANES], (LANES, LANES))
            return pltpu.roll(a, 0, 1, stride=1, stride_axis=0)

        e_lo = -(2 * nb - 1)
        r_prev = rotated(half + e_lo - 1)
        for e in range(e_lo, 2 * nb):
            r_cur = rotated(half + e)
            g = jnp.where(upper, r_cur, r_prev).astype(BF16)
            r_prev = r_cur
            if e % 2 == 0:
                d = e // 2
                t_ref[d + nb - 1, 0:LANES, 0:LANES] = g
                t_ref[d + nb - 1, LANES:TOEP, LANES:TOEP] = g
            else:
                d = (e - 1) // 2
                if abs(d) <= nb - 1:
                    t_ref[d + nb - 1, 0:LANES, LANES:TOEP] = g
                d = (e + 1) // 2
                if abs(d) <= nb - 1:
                    t_ref[d + nb - 1, LANES:TOEP, 0:LANES] = g
        for j in range(nb):
            u2_ref[j * batch:(j + 1) * batch, :] = u_ref[c, :, j * TOEP:(j + 1) * TOEP]
        y2_ref[...] = _dot(u2_ref[...], t_ref[nb - 1])
        for d in list(range(1, nb)) + list(range(-(nb - 1), 0)):
            j_lo, j_hi = max(0, -d), nb - max(0, d)
            res = _dot(u2_ref[j_lo * batch:j_hi * batch, :], t_ref[d + nb - 1])
            y2_ref[(j_lo + d) * batch:(j_hi + d) * batch, :] += res
        for j in range(nb):
            sl = slice(j * TOEP, (j + 1) * TOEP)
            out_ref[c, :, sl] = (xg_ref[c, :, sl].astype(F32) * y2_ref[j * batch:(j + 1) * batch, :]).astype(BF16)

    for c in range(cblk):
        for s, seq in enumerate(seqs):
            t_all, u2_all, y2_all = scr[3 * s:3 * s + 3]
            channel(c, seq, *ins[3 * s:3 * s + 3], outs[s], t_all.at[c], u2_all.at[c], y2_all.at[c])


def _hyena(hys, kexts, batch, seqs):
    cblk = 8
    args, in_specs, out_specs, out_shape, scratch = [], [], [], [], []
    for hy, kext, seq in zip(hys, kexts, seqs):
        nb = seq // TOEP
        hy4 = hy.reshape(2, HY_W, batch, seq)
        part = lambda p, seq=seq: pl.BlockSpec((None, cblk, batch, seq), lambda i: (p, i, 0, 0))
        args += [hy4, hy4, kext]
        in_specs += [part(0), part(1), pl.BlockSpec((cblk, 2 * seq), lambda i: (i, 0))]
        out_specs.append(pl.BlockSpec((cblk, batch, seq), lambda i: (i, 0, 0)))
        out_shape.append(jax.ShapeDtypeStruct((HY_W, batch, seq), BF16))
        scratch += [pltpu.VMEM((cblk, 2 * nb - 1, TOEP, TOEP), BF16),
                    pltpu.VMEM((cblk, nb * batch, TOEP), BF16),
                    pltpu.VMEM((cblk, nb * batch, TOEP), F32)]
    return pl.pallas_call(
        functools.partial(_hyena_kernel, seqs=tuple(seqs), batch=batch),
        grid=(HY_W // cblk,),
        in_specs=in_specs,
        out_specs=out_specs,
        out_shape=out_shape,
        scratch_shapes=scratch,
        compiler_params=_params(("arbitrary",)),
        name="hyena",
    )(*args)


def _retention_tables_kernel(af_ref, ab_ref, tab_ref):
    k_scale = RET_HD ** -0.5
    ri = lax.broadcasted_iota(jnp.int32, (CHUNK, CHUNK), 0).astype(F32)
    ci = lax.broadcasted_iota(jnp.int32, (CHUNK, CHUNK), 1).astype(F32)
    diff = ri - ci
    for h in range(RET_H):
        log_f = -jnp.exp(jnp.full((CHUNK, CHUNK), af_ref[h], F32))
        log_b = -jnp.exp(jnp.full((CHUNK, CHUNK), ab_ref[h], F32))
        dec_f = jnp.where(diff >= 0, jnp.exp(log_f * jnp.maximum(diff, 0.0)), 0.0)
        dec_b = jnp.where(diff <= 0, jnp.exp(log_b * jnp.maximum(-diff, 0.0)), 0.0)
        tab_ref[h, T_DEC] = (dec_f + dec_b) * k_scale
        tab_ref[h, T_QDF] = jnp.exp(log_f * (ri + 1.0))
        tab_ref[h, T_QDB] = jnp.exp(log_b * (CHUNK - ri))
        tab_ref[h, T_KDF] = jnp.exp(log_f * (CHUNK - 1.0 - ri)) * k_scale
        tab_ref[h, T_KDB] = jnp.exp(log_b * ri) * k_scale
        tab_ref[h, T_CDF] = jnp.exp(log_f * CHUNK)
        tab_ref[h, T_CDB] = jnp.exp(log_b * CHUNK)


def _retention_tables(a_f, a_b):
    smem = pl.BlockSpec(memory_space=pltpu.SMEM)
    return pl.pallas_call(
        _retention_tables_kernel,
        in_specs=[smem, smem],
        out_shape=jax.ShapeDtypeStruct((RET_H, N_TABLES, CHUNK, CHUNK), F32),
        name="retention_tables",
    )(a_f, a_b)


def _retention_kernel(tab_ref, zc_ref, zx_ref, ng_ref, oc_ref, ox_ref, sf_ref, sb_ref, sbc_ref, sbx_ref):
    hd = RET_HD
    sf_ref[...] = jnp.zeros_like(sf_ref)
    sb_ref[...] = jnp.zeros_like(sb_ref)

    def head_cols(part, h):
        return slice((part * RET_H + h) * hd, (part * RET_H + h + 1) * hd)

    def state_update(s, k, v, kd, cd):
        kdt = jnp.transpose(k.astype(F32) * kd).astype(BF16)
        return s * cd + _dot(kdt, v)

    def state_pass(z_ref, store_ref):
        nchunk = z_ref.shape[0] // CHUNK

        for i in range(nchunk):
            for t, s_ref, lo, t_kd, t_cd in ((i, sf_ref, 0, T_KDF, T_CDF),
                                             (nchunk - 1 - i, sb_ref, hd, T_KDB, T_CDB)):
                rows = slice(t * CHUNK, (t + 1) * CHUNK)
                for h in range(RET_H):
                    k = z_ref[rows, head_cols(1, h)]
                    v = z_ref[rows, head_cols(2, h)]
                    s = s_ref[h]
                    store_ref[t, h, :, lo:lo + hd] = s.astype(BF16)
                    s_ref[h] = state_update(s, k, v, tab_ref[h, t_kd], tab_ref[h, t_cd])

    def out_pass(z_ref, store_ref, out_ref):
        nchunk = z_ref.shape[0] // CHUNK

        for t in range(nchunk):
            rows = slice(t * CHUNK, (t + 1) * CHUNK)
            for h in range(RET_H):
                q = z_ref[rows, head_cols(0, h)]
                k = z_ref[rows, head_cols(1, h)]
                v = z_ref[rows, head_cols(2, h)]
                scores = _dot_nt(q, k) * tab_ref[h, T_DEC]
                cross = _dot(q, store_ref[t, h])
                o = (_dot(scores.astype(BF16), v) + cross[:, :hd] * tab_ref[h, T_QDF]
                     + cross[:, hd:] * tab_ref[h, T_QDB])
                mu = jnp.mean(o, axis=-1, keepdims=True)
                oc = o - mu
                var = jnp.mean(oc * oc, axis=-1, keepdims=True)
                y = oc * lax.rsqrt(var + EPS) * ng_ref[:, h * hd:(h + 1) * hd]
                gate = _silu(z_ref[rows, head_cols(3, h)].astype(F32))
                out_ref[rows, h * hd:(h + 1) * hd] = (y * gate).astype(BF16)

    state_pass(zc_ref, sbc_ref)
    state_pass(zx_ref, sbx_ref)
    out_pass(zc_ref, sbc_ref, oc_ref)
    out_pass(zx_ref, sbx_ref, ox_ref)


def _retention(zret_c, zret_x, a_f, a_b, norm_g, batch, ctx_len, seq):
    hd = RET_HD
    tables = _retention_tables(a_f, a_b)
    return pl.pallas_call(
        _retention_kernel,
        grid=(batch,),
        in_specs=[pl.BlockSpec(tables.shape, lambda b: (0, 0, 0, 0)),
                  pl.BlockSpec((ctx_len, 4 * RET_W), lambda b: (b, 0)),
                  pl.BlockSpec((seq, 4 * RET_W), lambda b: (b, 0)),
                  pl.BlockSpec((1, RET_W), lambda b: (0, 0))],
        out_specs=[pl.BlockSpec((ctx_len, RET_W), lambda b: (b, 0)),
                   pl.BlockSpec((seq, RET_W), lambda b: (b, 0))],
        out_shape=[jax.ShapeDtypeStruct((batch * ctx_len, RET_W), BF16),
                   jax.ShapeDtypeStruct((batch * seq, RET_W), BF16)],
        scratch_shapes=[
            pltpu.VMEM((RET_H, hd, hd), F32),
            pltpu.VMEM((RET_H, hd, hd), F32),
            pltpu.VMEM((ctx_len // CHUNK, RET_H, hd, 2 * hd), BF16),
            pltpu.VMEM((seq // CHUNK, RET_H, hd, 2 * hd), BF16),
        ],
        compiler_params=_params(("arbitrary",)),
        name="retention",
    )(tables, zret_c, zret_x, norm_g.reshape(1, RET_W))


def _head_norm_rope(z, gain, cos, sin):
    ms = jnp.mean(z * z, axis=-1, keepdims=True)
    y = z * lax.rsqrt(ms + EPS) * gain
    if cos is None:
        return y
    return y * cos + pltpu.roll(y, HEAD_DIM // 2, 1) * sin


def _store_v_aug(v_ref, v):
    ones = jnp.ones((v.shape[0], HEAD_DIM), BF16)
    for hd in range(N_KV):
        v_ref[:, 2 * hd * HEAD_DIM:(2 * hd + 1) * HEAD_DIM] = v[:, hd * HEAD_DIM:(hd + 1) * HEAD_DIM].astype(BF16)
        v_ref[:, (2 * hd + 1) * HEAD_DIM:(2 * hd + 2) * HEAD_DIM] = ones


def _mid_kernel(x_ref, yhyt_ref, yret_ref, mod0_ref, mod1_ref, wo_hy_ref, wo_ret_ref, g1_ref, win_ref,
                qn_ref, kn_ref, *rest, latent):
    d = x_ref.shape[-1]
    if latent:
        cos_ref, sin_ref, xo_ref, q_ref, k_ref, v_ref, g_ref = rest
    else:
        k_ref, v_ref = rest
    tm = x_ref.shape[0]
    sub = min(tm, MID_ROWS)
    for r in range(0, tm, sub):
        rs = slice(r, r + sub)
        cos, sin = (cos_ref[rs, :], sin_ref[rs, :]) if latent else (None, None)
        y = _dot_tn(yhyt_ref[:, rs], wo_hy_ref[...]) + _dot(yret_ref[rs, :], wo_ret_ref[...])
        x = x_ref[rs, :] + mod0_ref[:, 2 * d:3 * d] * y
        if latent:
            xo_ref[rs, :] = x
        h = _mod_norm(x, g1_ref[...], mod1_ref[:, 0:d], mod1_ref[:, d:2 * d]).astype(BF16)
        off = 0
        pair = 2 * HEAD_DIM
        if latent:
            for hp in range(N_HEADS // 2):
                z2 = _dot(h, win_ref[:, off:off + pair])
                for i in range(2):
                    q = _head_norm_rope(z2[:, i * HEAD_DIM:(i + 1) * HEAD_DIM], qn_ref[...], cos, sin) * Q_PRESCALE
                    q_ref[rs, off + i * HEAD_DIM:off + (i + 1) * HEAD_DIM] = q.astype(BF16)
                off += pair
        z2 = _dot(h, win_ref[:, off:off + pair])
        for i in range(N_KV):
            k = _head_norm_rope(z2[:, i * HEAD_DIM:(i + 1) * HEAD_DIM], kn_ref[...], cos, sin)
            k_ref[rs, i * HEAD_DIM:(i + 1) * HEAD_DIM] = k.astype(BF16)
        off += pair
        _store_v_aug(v_ref.at[rs, :], _dot(h, win_ref[:, off:off + KV_W]))
        off += KV_W
        if latent:
            g_ref[rs, :] = _dot(h, win_ref[:, off:off + ATT_W]).astype(BF16)


def _mid(x2, yhyt, yret, mod4, mod_row, wo_hy, wo_ret, g1, win, qn, kn, cosf, sinf, batch, seq, tm, latent):
    d = x2.shape[-1]
    nt = seq // tm
    tok = lambda w: pl.BlockSpec((tm, w), lambda b, t: (b * nt + t, 0))
    const = lambda a: pl.BlockSpec(a.shape, lambda b, t: (0,) * a.ndim)
    modspec = lambda layer: pl.BlockSpec((None, None, 1, 3 * d), lambda b, t: (layer, mod_row(b), 0, 0))
    in_specs = [tok(d), pl.BlockSpec((HY_W, tm), lambda b, t: (0, b * nt + t)), tok(RET_W),
                modspec(0), modspec(1), const(wo_hy), const(wo_ret), const(g1), const(win), const(qn), const(kn)]
    args = [x2, yhyt, yret, mod4, mod4, wo_hy, wo_ret, g1, win, qn, kn]
    n = batch * seq
    if latent:
        in_specs += [pl.BlockSpec((tm, HEAD_DIM), lambda b, t: (t, 0))] * 2
        args += [cosf, sinf]
        out_specs = [tok(d), tok(ATT_W), tok(KV_W), tok(2 * KV_W), tok(ATT_W)]
        out_shape = [jax.ShapeDtypeStruct((n, d), F32), jax.ShapeDtypeStruct((n, ATT_W), BF16),
                     jax.ShapeDtypeStruct((n, KV_W), BF16), jax.ShapeDtypeStruct((n, 2 * KV_W), BF16),
                     jax.ShapeDtypeStruct((n, ATT_W), BF16)]
    else:
        out_specs = [tok(KV_W), tok(2 * KV_W)]
        out_shape = [jax.ShapeDtypeStruct((n, KV_W), BF16), jax.ShapeDtypeStruct((n, 2 * KV_W), BF16)]
    return pl.pallas_call(
        functools.partial(_mid_kernel, latent=latent),
        grid=(batch, nt),
        in_specs=in_specs,
        out_specs=out_specs,
        out_shape=out_shape,
        compiler_params=_params(("arbitrary", "arbitrary")),
        name="mid_latent" if latent else "mid_ctx",
    )(*args)


def _attn_kernel(q_ref, kc_ref, vc_ref, kx_ref, vx_ref, g_ref, x_ref, mod_ref, w_ref, fg_ref, out_ref):
    d = x_ref.shape[-1]
    heads = []
    for hd in range(N_HEADS):
        kvh = hd // GROUP
        ksl = slice(kvh * HEAD_DIM, (kvh + 1) * HEAD_DIM)
        vsl = slice(2 * kvh * HEAD_DIM, 2 * (kvh + 1) * HEAD_DIM)
        sl = slice(hd * HEAD_DIM, (hd + 1) * HEAD_DIM)
        q = q_ref[:, sl]
        s_c = _dot_nt(q, kc_ref[:, ksl])
        s_x = _dot_nt(q, kx_ref[:, ksl])
        m = jnp.maximum(jnp.max(s_c, axis=-1, keepdims=True), jnp.max(s_x, axis=-1, keepdims=True))
        p_c = jnp.exp2(s_c - m).astype(BF16)
        p_x = jnp.exp2(s_x - m).astype(BF16)
        o = _dot(p_c, vc_ref[:, vsl]) + _dot(p_x, vx_ref[:, vsl])
        gate = _silu(g_ref[:, sl].astype(F32))
        heads.append((gate * (o[:, :HEAD_DIM] / o[:, HEAD_DIM:])).astype(BF16))
    y = _dot(jnp.concatenate(heads, axis=1), w_ref[...])
    x = x_ref[...] + mod_ref[:, 2 * d:3 * d] * y
    ms = jnp.mean(x * x, axis=-1, keepdims=True)
    out_ref[...] = x * lax.rsqrt(ms + EPS) * fg_ref[...]


def _attention(q, kc, vc, kx, vx, g, x2, mod4, w, fg, batch, ctx_len, seq, tq):
    d = x2.shape[-1]
    nt = seq // tq
    kv = lambda rows, w_: pl.BlockSpec((rows, w_), lambda b, t: (b, 0))
    tok = lambda w_: pl.BlockSpec((tq, w_), lambda b, t: (b * nt + t, 0))
    const = lambda a: pl.BlockSpec(a.shape, lambda b, t: (0,) * a.ndim)
    return pl.pallas_call(
        _attn_kernel,
        grid=(batch, nt),
        in_specs=[tok(ATT_W), kv(ctx_len, KV_W), kv(ctx_len, 2 * KV_W), kv(seq, KV_W), kv(seq, 2 * KV_W),
                  tok(ATT_W), tok(d),
                  pl.BlockSpec((None, None, 1, 3 * d), lambda b, t: (1, b, 0, 0)), const(w), const(fg)],
        out_specs=tok(d),
        out_shape=jax.ShapeDtypeStruct((batch * seq, d), F32),
        compiler_params=_params(("arbitrary", "arbitrary")),
        name="attention",
    )(q, kc, vc, kx, vx, g, x2, mod4, w, fg)


def _rope_tables(seq):
    rows = seq // GRID_W
    r = jnp.repeat(jnp.arange(rows, dtype=F32), GRID_W)
    col = jnp.tile(jnp.arange(GRID_W, dtype=F32), rows)
    half = HEAD_DIM // 2
    inv = ROPE_THETA ** (-jnp.arange(0, half, 2, dtype=F32) / half)
    ang = jnp.concatenate([r[:, None] * inv, col[:, None] * inv], axis=-1)
    cos, sin = jnp.cos(ang), jnp.sin(ang)
    return jnp.concatenate([cos, cos], axis=-1), jnp.concatenate([-sin, sin], axis=-1)


def kernel(x, c, ctx, c_ctx, norm_g, ada_w, ada_b, er_in_w, er_out_w, hy_conv_w, hy_conv_b, hy_f_w1, hy_f_b1, hy_f_freq, hy_f_w2, hy_f_b2, hy_f_w3, hy_bias, ret_decay_f, ret_decay_b, ret_norm_g, at_in_w, at_out_w, at_q_norm, at_k_norm, final_norm_g):
    batch, seq, d = x.shape
    ctx_len = ctx.shape[1]
    assert seq % TOEP == 0 and ctx_len % TOEP == 0 and seq % GRID_W == 0 and d == 2 * HY_W
    tm = min(512, seq)

    rows = -(-(batch + 1) // 8) * 8
    cond = jnp.concatenate([c, c_ctx[None, :], jnp.zeros((rows - batch - 1, d), F32)], axis=0)
    mod4 = _ada(cond, ada_w, ada_b).reshape(ada_w.shape[0], rows, 1, 3 * d)
    lat_row = lambda b: b
    ctx_row = lambda b: batch

    x2 = x.reshape(batch * seq, d)
    c2 = ctx.reshape(batch * ctx_len, d)

    in_w = er_in_w[0].astype(BF16)
    col = lambda p: in_w[:, p * HY_W:(p + 1) * HY_W].T
    w_hy = jnp.stack([jnp.concatenate([col(0), col(3)], axis=0), jnp.concatenate([col(1), col(2)], axis=0)])
    w_ret = in_w[:, 4 * HY_W:]
    cw = hy_conv_w[0].T
    cb = hy_conv_b[0].reshape(3 * HY_W, 1)
    g0 = norm_g[0].reshape(1, d)
    hy_x, h_x = _inproj_hy(x2, mod4, 0, lat_row, g0, w_hy, cw, cb, batch, seq, seq)
    ctx_rows = math.gcd(batch * ctx_len, seq)
    hy_c, h_c = _inproj_hy(c2, mod4, 0, ctx_row, g0, w_hy, cw, cb, batch * ctx_len // ctx_rows, ctx_rows, ctx_len)
    zret_x = _inproj_ret(h_x, w_ret, min(2 * tm, h_x.shape[0]))
    zret_c = _inproj_ret(h_c, w_ret, min(2 * tm, h_c.shape[0]))

    filt = (hy_f_w1[0], hy_f_b1[0], hy_f_freq[0], hy_f_w2[0], hy_f_b2[0], hy_f_w3[0], hy_bias[0])
    yhyt_x, yhyt_c = _hyena((hy_x, hy_c), (_filters(seq, *filt), _filters(ctx_len, *filt)), batch, (seq, ctx_len))
    yret_c, yret_x = _retention(zret_c, zret_x, ret_decay_f[0], ret_decay_b[0], ret_norm_g[0], batch, ctx_len, seq)

    wo_hy = er_out_w[0][:HY_W].astype(BF16)
    wo_ret = er_out_w[0][HY_W:].astype(BF16)
    perm = jnp.concatenate([jnp.arange(0, HEAD_DIM, 2), jnp.arange(1, HEAD_DIM, 2)])
    qk_cols = (jnp.arange(N_HEADS + N_KV)[:, None] * HEAD_DIM + perm[None, :]).reshape(-1)
    w1 = at_in_w[0]
    win_x = jnp.concatenate([w1[:, qk_cols], w1[:, ATT_W + KV_W:]], axis=1).astype(BF16)
    win_c = win_x[:, ATT_W:ATT_W + 2 * KV_W]
    qn = at_q_norm[0][perm].reshape(1, HEAD_DIM)
    kn = at_k_norm[0][perm].reshape(1, HEAD_DIM)
    g1 = norm_g[1].reshape(1, d)
    cosf, sinf = _rope_tables(seq)
    x1, q, kx, vx, gate = _mid(x2, yhyt_x.reshape(HY_W, batch * seq), yret_x, mod4, lat_row, wo_hy, wo_ret, g1,
                               win_x, qn, kn, cosf, sinf, batch, seq, tm, True)
    kc, vc = _mid(c2, yhyt_c.reshape(HY_W, batch * ctx_len), yret_c, mod4, ctx_row, wo_hy, wo_ret, g1,
                  win_c, qn, kn, None, None, batch, ctx_len, ctx_len, False)

    out = _attention(q, kc, vc, kx, vx, gate, x1, mod4, at_out_w[0].astype(BF16), final_norm_g.reshape(1, d),
                     batch, ctx_len, seq, min(1024, seq))
    return out.reshape(batch, seq, d)
```

```python
import functools
import math

import jax
import jax.numpy as jnp
from jax import lax
from jax.experimental import pallas as pl
from jax.experimental.pallas import tpu as pltpu

F32 = jnp.float32
BF16 = jnp.bfloat16
HIGHEST = lax.Precision.HIGHEST

EPS = 1e-6
GRID_W = 64
HY_W = 512
HY_EMB = 33
HY_BANDS = (HY_EMB - 1) // 2
HY_ORDER = 64
HY_DECAY_TARGET = 1e-2
HY_FAST_PCT = 0.3
HY_SLOW_PCT = 1.5
RET_W = 512
RET_H = 4
RET_HD = RET_W // RET_H
CHUNK = 128
N_HEADS = 8
N_KV = 2
HEAD_DIM = 128
GROUP = N_HEADS // N_KV
ATT_W = N_HEADS * HEAD_DIM
KV_W = N_KV * HEAD_DIM
ROPE_THETA = 10000.0
Q_PRESCALE = HEAD_DIM ** -0.5 * math.log2(math.e)

MID_ROWS = 256
HY_CH = 256
LANES = 128
TOEP = 256
VMEM_LIMIT = 56 * 1024 * 1024
T_DEC, T_QDF, T_QDB, T_KDF, T_KDB, T_CDF, T_CDB = range(7)
N_TABLES = 7


def _params(sem):
    return pltpu.CompilerParams(dimension_semantics=sem, vmem_limit_bytes=VMEM_LIMIT)


def _silu(x):
    return x * (1.0 / (1.0 + jnp.exp(-x)))


def _dot(a, b):
    return jnp.dot(a, b, preferred_element_type=F32)


def _dot_nt(a, b):
    return lax.dot_general(a, b, (((1,), (1,)), ((), ())), preferred_element_type=F32)


def _dot_tn(a, b):
    return lax.dot_general(a, b, (((0,), (0,)), ((), ())), preferred_element_type=F32)


def _mod_norm(x, g, shift, scale):
    ms = jnp.mean(x * x, axis=-1, keepdims=True)
    return (x * lax.rsqrt(ms + EPS) * g) * (1.0 + scale) + shift


def _ada_kernel(cond_ref, w_ref, b_ref, o_ref):
    s = _silu(cond_ref[...])
    o_ref[...] = jnp.dot(s, w_ref[...], preferred_element_type=F32, precision=HIGHEST) + b_ref[...]


def _ada(cond, ada_w, ada_b):
    depth, d, n = ada_w.shape
    rows = cond.shape[0]
    tn = 1024
    return pl.pallas_call(
        _ada_kernel,
        grid=(depth, n // tn),
        in_specs=[
            pl.BlockSpec((rows, d), lambda i, j: (0, 0)),
            pl.BlockSpec((None, d, tn), lambda i, j: (i, 0, j)),
            pl.BlockSpec((None, 1, tn), lambda i, j: (i, 0, j)),
        ],
        out_specs=pl.BlockSpec((None, rows, tn), lambda i, j: (i, 0, j)),
        out_shape=jax.ShapeDtypeStruct((depth, rows, n), F32),
        compiler_params=_params(("arbitrary", "arbitrary")),
        name="ada",
    )(cond, ada_w, ada_b.reshape(depth, 1, n))


def _inproj_hy_kernel(x_ref, mod_ref, g_ref, w_ref, cw_ref, cb_ref, hy_ref, h_ref, *, period):
    seq, d = x_ref.shape
    rows = min(seq, 2 * MID_ROWS)
    for r in range(0, seq, rows):
        rs = slice(r, r + rows)
        h_ref[rs, :] = _mod_norm(x_ref[rs, :], g_ref[...], mod_ref[:, 0:d], mod_ref[:, d:2 * d]).astype(BF16)

    lane = lax.broadcasted_iota(jnp.int32, (1, seq), 1)
    first = lane % period == 0
    last = lane % period == period - 1

    def conv(a, part, c0):
        ch = slice(part * HY_W + c0, part * HY_W + c0 + a.shape[0])
        left = jnp.where(first, 0.0, pltpu.roll(a, 1, 1))
        right = jnp.where(last, 0.0, pltpu.roll(a, seq - 1, 1))
        return left * cw_ref[ch, 0:1] + a * cw_ref[ch, 1:2] + right * cw_ref[ch, 2:3] + cb_ref[ch, :]

    def hyena_part(p, combine):
        h = h_ref[...]
        sub = 8
        for c0 in range(0, HY_W, HY_CH):
            za = _dot_nt(w_ref[p, c0:c0 + HY_CH, :], h)
            zb = _dot_nt(w_ref[p, HY_W + c0:HY_W + c0 + HY_CH, :], h)
            for r in range(0, HY_CH, 2 * sub):
                pieces = [combine(za[r + i:r + i + sub, :], zb[r + i:r + i + sub, :], c0 + r + i) for i in (0, sub)]
                hy_ref[p, c0 + r:c0 + r + 2 * sub, :] = jnp.concatenate(pieces, axis=0).astype(BF16)

    hyena_part(0, lambda x0, gate, c0: conv(x0, 0, c0) * _silu(gate))
    hyena_part(1, lambda x1, v, c0: conv(x1, 1, c0) * conv(v, 2, c0))


def _inproj_hy(x2, mod4, layer, mod_row, g, w_hy, cw, cb, batch, seq, period):
    d = x2.shape[-1]
    return pl.pallas_call(
        functools.partial(_inproj_hy_kernel, period=period),
        grid=(batch,),
        in_specs=[
            pl.BlockSpec((seq, d), lambda b: (b, 0)),
            pl.BlockSpec((None, None, 1, 3 * d), lambda b: (layer, mod_row(b), 0, 0)),
            pl.BlockSpec((1, d), lambda b: (0, 0)),
            pl.BlockSpec(w_hy.shape, lambda b: (0, 0, 0), pipeline_mode=pl.Buffered(1)),
            pl.BlockSpec(cw.shape, lambda b: (0, 0), pipeline_mode=pl.Buffered(1)),
            pl.BlockSpec(cb.shape, lambda b: (0, 0), pipeline_mode=pl.Buffered(1)),
        ],
        out_specs=[
            pl.BlockSpec((2, HY_W, seq), lambda b: (0, 0, b)),
            pl.BlockSpec((seq, d), lambda b: (b, 0)),
        ],
        out_shape=[
            jax.ShapeDtypeStruct((2, HY_W, batch * seq), BF16),
            jax.ShapeDtypeStruct((batch * seq, d), BF16),
        ],
        compiler_params=_params(("arbitrary",)),
        name="inproj_hy",
    )(x2, mod4, g, w_hy, cw, cb)


def _inproj_ret_kernel(h_ref, w_ref, z_ref):
    n = w_ref.shape[1]
    step = 2 * RET_W
    for c in range(0, n, step):
        z_ref[:, c:c + step] = _dot(h_ref[...], w_ref[:, c:c + step]).astype(BF16)


def _inproj_ret(h, w_ret, tm):
    n, d = h.shape
    nr = w_ret.shape[1]
    return pl.pallas_call(
        _inproj_ret_kernel,
        grid=(n // tm,),
        in_specs=[pl.BlockSpec((tm, d), lambda i: (i, 0)), pl.BlockSpec((d, nr), lambda i: (0, 0))],
        out_specs=pl.BlockSpec((tm, nr), lambda i: (i, 0)),
        out_shape=jax.ShapeDtypeStruct((n, nr), BF16),
        compiler_params=_params(("arbitrary",)),
        name="inproj_ret",
    )(h, w_ret)


def _filter_kernel(w1t_ref, w1c_ref, w1s_ref, b1_ref, f_ref, w2_ref, b2_ref, w3_ref, bias_ref, o_ref, *, seq):
    two_l = o_ref.shape[-1]
    cb = o_ref.shape[0]
    lag = lax.broadcasted_iota(jnp.int32, (1, two_l), 1) - seq
    n = jnp.abs(lag).astype(F32)
    t = n / float(max(seq - 1, 1))
    band_i = lax.broadcasted_iota(jnp.int32, (HY_BANDS, 1), 0).astype(F32)
    bands = 1e-4 + band_i * ((HY_BANDS - 1 - 1e-4) / (HY_BANDS - 1))
    ang = (2.0 * math.pi / seq) * n * bands
    f = f_ref[...]
    pre = (w1t_ref[...] * t
           + jnp.dot(w1c_ref[...], jnp.cos(ang), preferred_element_type=F32, precision=HIGHEST)
           + jnp.dot(w1s_ref[...], -jnp.sin(ang), preferred_element_type=F32, precision=HIGHEST))
    h = jnp.sin(f * (pre + b1_ref[...]))
    h = jnp.sin(f * (jnp.dot(w2_ref[...], h, preferred_element_type=F32, precision=HIGHEST) + b2_ref[...]))
    h_b = jnp.dot(w3_ref[1], h[:, :seq], preferred_element_type=F32, precision=HIGHEST)
    h_f = jnp.dot(w3_ref[0], h[:, seq:], preferred_element_type=F32, precision=HIGHEST)
    max_decay = math.log(HY_DECAY_TARGET) / HY_FAST_PCT
    min_decay = math.log(HY_DECAY_TARGET) / HY_SLOW_PCT
    ch = (lax.broadcasted_iota(jnp.int32, (cb, 1), 0) + pl.program_id(0) * cb).astype(F32)
    deltas = jnp.abs(min_decay + ch * ((max_decay - min_decay) / (HY_W - 1)))
    window = jnp.exp(-t * deltas)
    k = jnp.concatenate([h_b, h_f], axis=1) * window
    k = jnp.where(lag == -seq, 0.0, k)
    o_ref[...] = k + jnp.where(lag == 0, bias_ref[...], 0.0)


def _filters(seq, w1, b1, freq, w2, b2, w3, bias):
    cb = 128
    w1t = w1.T
    args = (
        w1t[:, 0:1], w1t[:, 1:1 + HY_BANDS], w1t[:, 1 + HY_BANDS:],
        b1.reshape(HY_ORDER, 1), freq.reshape(HY_ORDER, 1), w2.T, b2.reshape(HY_ORDER, 1),
        w3.T.reshape(2, HY_W, HY_ORDER), bias.reshape(HY_W, 1),
    )
    full = lambda a: pl.BlockSpec(a.shape, lambda i: (0,) * a.ndim)
    in_specs = [full(a) for a in args[:7]] + [
        pl.BlockSpec((2, cb, HY_ORDER), lambda i: (0, i, 0)),
        pl.BlockSpec((cb, 1), lambda i: (i, 0)),
    ]
    return pl.pallas_call(
        functools.partial(_filter_kernel, seq=seq),
        grid=(HY_W // cb,),
        in_specs=in_specs,
        out_specs=pl.BlockSpec((cb, 2 * seq), lambda i: (i, 0)),
        out_shape=jax.ShapeDtypeStruct((HY_W, 2 * seq), F32),
        compiler_params=_params(("arbitrary",)),
        name="hyena_filters",
    )(*args)


def _hyena_kernel(*refs, seqs, batch):
    ns = len(seqs)
    ins, outs, scr = refs[:3 * ns], refs[3 * ns:4 * ns], refs[4 * ns:]
    cblk = outs[0].shape[0]
    row_i = lax.broadcasted_iota(jnp.int32, (LANES, LANES), 0)
    col_i = lax.broadcasted_iota(jnp.int32, (LANES, LANES), 1)
    upper = col_i >= row_i

    def channel(c, seq, xg_ref, u_ref, kext_ref, out_ref, t_ref, u2_ref, y2_ref):
        nb = seq // TOEP
        half = seq // LANES
        krow = kext_ref[c:c + 1, :]

        def rotated(q):
            a = jnp.broadcast_to(krow[:, q * LANES:(q + 1) * LANES], (LANES, LANES))
            return pltpu.roll(a, 0, 1, stride=1, stride_axis=0)

        e_lo = -(2 * nb - 1)
        r_prev = rotated(half + e_lo - 1)
        for e in range(e_lo, 2 * nb):
            r_cur = rotated(half + e)
            g = jnp.where(upper, r_cur, r_prev).astype(BF16)
            r_prev = r_cur
            if e % 2 == 0:
                d = e // 2
                t_ref[d + nb - 1, 0:LANES, 0:LANES] = g
                t_ref[d + nb - 1, LANES:TOEP, LANES:TOEP] = g
            else:
                d = (e - 1) // 2
                if abs(d) <= nb - 1:
                    t_ref[d + nb - 1, 0:LANES, LANES:TOEP] = g
                d = (e + 1) // 2
                if abs(d) <= nb - 1:
                    t_ref[d + nb - 1, LANES:TOEP, 0:LANES] = g
        for j in range(nb):
            u2_ref[j * batch:(j + 1) * batch, :] = u_ref[c, :, j * TOEP:(j + 1) * TOEP]
        y2_ref[...] = _dot(u2_ref[...], t_ref[nb - 1])
        for d in list(range(1, nb)) + list(range(-(nb - 1), 0)):
            j_lo, j_hi = max(0, -d), nb - max(0, d)
            res = _dot(u2_ref[j_lo * batch:j_hi * batch, :], t_ref[d + nb - 1])
            y2_ref[(j_lo + d) * batch:(j_hi + d) * batch, :] += res
        for j in range(nb):
            sl = slice(j * TOEP, (j + 1) * TOEP)
            out_ref[c, :, sl] = (xg_ref[c, :, sl].astype(F32) * y2_ref[j * batch:(j + 1) * batch, :]).astype(BF16)

    for c in range(cblk):
        for s, seq in enumerate(seqs):
            t_all, u2_all, y2_all = scr[3 * s:3 * s + 3]
            channel(c, seq, *ins[3 * s:3 * s + 3], outs[s], t_all.at[c], u2_all.at[c], y2_all.at[c])


def _hyena(hys, kexts, batch, seqs):
    cblk = 8
    args, in_specs, out_specs, out_shape, scratch = [], [], [], [], []
    for hy, kext, seq in zip(hys, kexts, seqs):
        nb = seq // TOEP
        hy4 = hy.reshape(2, HY_W, batch, seq)
        part = lambda p, seq=seq: pl.BlockSpec((None, cblk, batch, seq), lambda i: (p, i, 0, 0))
        args += [hy4, hy4, kext]
        in_specs += [part(0), part(1), pl.BlockSpec((cblk, 2 * seq), lambda i: (i, 0))]
        out_specs.append(pl.BlockSpec((cblk, batch, seq), lambda i: (i, 0, 0)))
        out_shape.append(jax.ShapeDtypeStruct((HY_W, batch, seq), BF16))
        scratch += [pltpu.VMEM((cblk, 2 * nb - 1, TOEP, TOEP), BF16),
                    pltpu.VMEM((cblk, nb * batch, TOEP), BF16),
                    pltpu.VMEM((cblk, nb * batch, TOEP), F32)]
    return pl.pallas_call(
        functools.partial(_hyena_kernel, seqs=tuple(seqs), batch=batch),
        grid=(HY_W // cblk,),
        in_specs=in_specs,
        out_specs=out_specs,
        out_shape=out_shape,
        scratch_shapes=scratch,
        compiler_params=_params(("arbitrary",)),
        name="hyena",
    )(*args)


def _retention_tables_kernel(af_ref, ab_ref, tab_ref):
    k_scale = RET_HD ** -0.5
    ri = lax.broadcasted_iota(jnp.int32, (CHUNK, CHUNK), 0).astype(F32)
    ci = lax.broadcasted_iota(jnp.int32, (CHUNK, CHUNK), 1).astype(F32)
    diff = ri - ci
    for h in range(RET_H):
        log_f = -jnp.exp(jnp.full((CHUNK, CHUNK), af_ref[h], F32))
        log_b = -jnp.exp(jnp.full((CHUNK, CHUNK), ab_ref[h], F32))
        dec_f = jnp.where(diff >= 0, jnp.exp(log_f * jnp.maximum(diff, 0.0)), 0.0)
        dec_b = jnp.where(diff <= 0, jnp.exp(log_b * jnp.maximum(-diff, 0.0)), 0.0)
        tab_ref[h, T_DEC] = (dec_f + dec_b) * k_scale
        tab_ref[h, T_QDF] = jnp.exp(log_f * (ri + 1.0))
        tab_ref[h, T_QDB] = jnp.exp(log_b * (CHUNK - ri))
        tab_ref[h, T_KDF] = jnp.exp(log_f * (CHUNK - 1.0 - ri)) * k_scale
        tab_ref[h, T_KDB] = jnp.exp(log_b * ri) * k_scale
        tab_ref[h, T_CDF] = jnp.exp(log_f * CHUNK)
        tab_ref[h, T_CDB] = jnp.exp(log_b * CHUNK)


def _retention_tables(a_f, a_b):
    smem = pl.BlockSpec(memory_space=pltpu.SMEM)
    return pl.pallas_call(
        _retention_tables_kernel,
        in_specs=[smem, smem],
        out_shape=jax.ShapeDtypeStruct((RET_H, N_TABLES, CHUNK, CHUNK), F32),
        name="retention_tables",
    )(a_f, a_b)


def _retention_kernel(tab_ref, zc_ref, zx_ref, ng_ref, oc_ref, ox_ref, sf_ref, sb_ref, sbc_ref, sbx_ref):
    hd = RET_HD
    sf_ref[...] = jnp.zeros_like(sf_ref)
    sb_ref[...] = jnp.zeros_like(sb_ref)

    def head_cols(part, h):
        return slice((part * RET_H + h) * hd, (part * RET_H + h + 1) * hd)

    def state_update(s, k, v, kd, cd):
        kdt = jnp.transpose(k.astype(F32) * kd).astype(BF16)
        return s * cd + _dot(kdt, v)

    def state_pass(z_ref, store_ref):
        nchunk = z_ref.shape[0] // CHUNK

        for i in range(nchunk):
            for t, s_ref, lo, t_kd, t_cd in ((i, sf_ref, 0, T_KDF, T_CDF),
                                             (nchunk - 1 - i, sb_ref, hd, T_KDB, T_CDB)):
                rows = slice(t * CHUNK, (t + 1) * CHUNK)
                for h in range(RET_H):
                    k = z_ref[rows, head_cols(1, h)]
                    v = z_ref[rows, head_cols(2, h)]
                    s = s_ref[h]
                    store_ref[t, h, :, lo:lo + hd] = s.astype(BF16)
                    s_ref[h] = state_update(s, k, v, tab_ref[h, t_kd], tab_ref[h, t_cd])

    def out_pass(z_ref, store_ref, out_ref):
        nchunk = z_ref.shape[0] // CHUNK

        for t in range(nchunk):
            rows = slice(t * CHUNK, (t + 1) * CHUNK)
            for h in range(RET_H):
                q = z_ref[rows, head_cols(0, h)]
                k = z_ref[rows, head_cols(1, h)]
                v = z_ref[rows, head_cols(2, h)]
                scores = _dot_nt(q, k) * tab_ref[h, T_DEC]
                cross = _dot(q, store_ref[t, h])
                o = (_dot(scores.astype(BF16), v) + cross[:, :hd] * tab_ref[h, T_QDF]
                     + cross[:, hd:] * tab_ref[h, T_QDB])
                mu = jnp.mean(o, axis=-1, keepdims=True)
                oc = o - mu
                var = jnp.mean(oc * oc, axis=-1, keepdims=True)
                y = oc * lax.rsqrt(var + EPS) * ng_ref[:, h * hd:(h + 1) * hd]
                gate = _silu(z_ref[rows, head_cols(3, h)].astype(F32))
                out_ref[rows, h * hd:(h + 1) * hd] = (y * gate).astype(BF16)

    state_pass(zc_ref, sbc_ref)
    state_pass(zx_ref, sbx_ref)
    out_pass(zc_ref, sbc_ref, oc_ref)
    out_pass(zx_ref, sbx_ref, ox_ref)


def _retention(zret_c, zret_x, a_f, a_b, norm_g, batch, ctx_len, seq):
    hd = RET_HD
    tables = _retention_tables(a_f, a_b)
    return pl.pallas_call(
        _retention_kernel,
        grid=(batch,),
        in_specs=[pl.BlockSpec(tables.shape, lambda b: (0, 0, 0, 0)),
                  pl.BlockSpec((ctx_len, 4 * RET_W), lambda b: (b, 0)),
                  pl.BlockSpec((seq, 4 * RET_W), lambda b: (b, 0)),
                  pl.BlockSpec((1, RET_W), lambda b: (0, 0))],
        out_specs=[pl.BlockSpec((ctx_len, RET_W), lambda b: (b, 0)),
                   pl.BlockSpec((seq, RET_W), lambda b: (b, 0))],
        out_shape=[jax.ShapeDtypeStruct((batch * ctx_len, RET_W), BF16),
                   jax.ShapeDtypeStruct((batch * seq, RET_W), BF16)],
        scratch_shapes=[
            pltpu.VMEM((RET_H, hd, hd), F32),
            pltpu.VMEM((RET_H, hd, hd), F32),
            pltpu.VMEM((ctx_len // CHUNK, RET_H, hd, 2 * hd), BF16),
            pltpu.VMEM((seq // CHUNK, RET_H, hd, 2 * hd), BF16),
        ],
        compiler_params=_params(("arbitrary",)),
        name="retention",
    )(tables, zret_c, zret_x, norm_g.reshape(1, RET_W))


def _head_norm_rope(z, gain, cos, sin):
    ms = jnp.mean(z * z, axis=-1, keepdims=True)
    y = z * lax.rsqrt(ms + EPS) * gain
    if cos is None:
        return y
    return y * cos + pltpu.roll(y, HEAD_DIM // 2, 1) * sin


def _store_v_aug(v_ref, v):
    ones = jnp.ones((v.shape[0], HEAD_DIM), BF16)
    for hd in range(N_KV):
        v_ref[:, 2 * hd * HEAD_DIM:(2 * hd + 1) * HEAD_DIM] = v[:, hd * HEAD_DIM:(hd + 1) * HEAD_DIM].astype(BF16)
        v_ref[:, (2 * hd + 1) * HEAD_DIM:(2 * hd + 2) * HEAD_DIM] = ones


def _mid_kernel(x_ref, yhyt_ref, yret_ref, mod0_ref, mod1_ref, wo_hy_ref, wo_ret_ref, g1_ref, win_ref,
                qn_ref, kn_ref, *rest, latent):
    d = x_ref.shape[-1]
    if latent:
        cos_ref, sin_ref, xo_ref, q_ref, k_ref, v_ref, g_ref = rest
    else:
        k_ref, v_ref = rest
    tm = x_ref.shape[0]
    sub = min(tm, MID_ROWS)
    pair = 2 * HEAD_DIM

    def front(rs):
        y = _dot_tn(yhyt_ref[:, rs], wo_hy_ref[...]) + _dot(yret_ref[rs, :], wo_ret_ref[...])
        x = x_ref[rs, :] + mod0_ref[:, 2 * d:3 * d] * y
        if latent:
            xo_ref[rs, :] = x
        return _mod_norm(x, g1_ref[...], mod1_ref[:, 0:d], mod1_ref[:, d:2 * d]).astype(BF16)

    def back(rs, h):
        cos, sin = (cos_ref[rs, :], sin_ref[rs, :]) if latent else (None, None)
        pieces = []

        def q_pair(off):
            z2 = _dot(h, win_ref[:, off:off + pair])
            for i in range(2):
                q = _head_norm_rope(z2[:, i * HEAD_DIM:(i + 1) * HEAD_DIM], qn_ref[...], cos, sin) * Q_PRESCALE
                q_ref[rs, off + i * HEAD_DIM:off + (i + 1) * HEAD_DIM] = q.astype(BF16)

        def k_pair(off):
            z2 = _dot(h, win_ref[:, off:off + pair])
            for i in range(N_KV):
                k = _head_norm_rope(z2[:, i * HEAD_DIM:(i + 1) * HEAD_DIM], kn_ref[...], cos, sin)
                k_ref[rs, i * HEAD_DIM:(i + 1) * HEAD_DIM] = k.astype(BF16)

        def v_part(off):
            _store_v_aug(v_ref.at[rs, :], _dot(h, win_ref[:, off:off + KV_W]))

        def g_part(off, c):
            g_ref[rs, c:c + pair] = _dot(h, win_ref[:, off + c:off + c + pair]).astype(BF16)

        off = 0
        if latent:
            for hp in range(N_HEADS // 2):
                pieces.append(functools.partial(q_pair, off))
                off += pair
        pieces.append(functools.partial(k_pair, off))
        off += pair
        pieces.append(functools.partial(v_part, off))
        off += KV_W
        if latent:
            for c in range(0, ATT_W, pair):
                pieces.append(functools.partial(g_part, off, c))
        return pieces

    groups = [slice(r, r + sub) for r in range(0, tm, sub)]
    h = front(groups[0])
    for gi, rs in enumerate(groups):
        pieces = back(rs, h)
        cut = len(pieces) // 4
        for piece in pieces[:cut]:
            piece()
        if gi + 1 < len(groups):
            h = front(groups[gi + 1])
        for piece in pieces[cut:]:
            piece()


def _mid(x2, yhyt, yret, mod4, mod_row, wo_hy, wo_ret, g1, win, qn, kn, cosf, sinf, batch, seq, tm, latent):
    d = x2.shape[-1]
    nt = seq // tm
    tok = lambda w: pl.BlockSpec((tm, w), lambda b, t: (b * nt + t, 0))
    const = lambda a: pl.BlockSpec(a.shape, lambda b, t: (0,) * a.ndim)
    modspec = lambda layer: pl.BlockSpec((None, None, 1, 3 * d), lambda b, t: (layer, mod_row(b), 0, 0))
    in_specs = [tok(d), pl.BlockSpec((HY_W, tm), lambda b, t: (0, b * nt + t)), tok(RET_W),
                modspec(0), modspec(1), const(wo_hy), const(wo_ret), const(g1), const(win), const(qn), const(kn)]
    args = [x2, yhyt, yret, mod4, mod4, wo_hy, wo_ret, g1, win, qn, kn]
    n = batch * seq
    if latent:
        in_specs += [pl.BlockSpec((tm, HEAD_DIM), lambda b, t: (t, 0))] * 2
        args += [cosf, sinf]
        out_specs = [tok(d), tok(ATT_W), tok(KV_W), tok(2 * KV_W), tok(ATT_W)]
        out_shape = [jax.ShapeDtypeStruct((n, d), F32), jax.ShapeDtypeStruct((n, ATT_W), BF16),
                     jax.ShapeDtypeStruct((n, KV_W), BF16), jax.ShapeDtypeStruct((n, 2 * KV_W), BF16),
                     jax.ShapeDtypeStruct((n, ATT_W), BF16)]
    else:
        out_specs = [tok(KV_W), tok(2 * KV_W)]
        out_shape = [jax.ShapeDtypeStruct((n, KV_W), BF16), jax.ShapeDtypeStruct((n, 2 * KV_W), BF16)]
    return pl.pallas_call(
        functools.partial(_mid_kernel, latent=latent),
        grid=(batch, nt),
        in_specs=in_specs,
        out_specs=out_specs,
        out_shape=out_shape,
        compiler_params=_params(("arbitrary", "arbitrary")),
        name="mid_latent" if latent else "mid_ctx",
    )(*args)


def _attn_kernel(q_ref, kc_ref, vc_ref, kx_ref, vx_ref, g_ref, x_ref, mod_ref, w_ref, fg_ref, out_ref):
    d = x_ref.shape[-1]
    heads = []
    for hd in range(N_HEADS):
        kvh = hd // GROUP
        ksl = slice(kvh * HEAD_DIM, (kvh + 1) * HEAD_DIM)
        vsl = slice(2 * kvh * HEAD_DIM, 2 * (kvh + 1) * HEAD_DIM)
        sl = slice(hd * HEAD_DIM, (hd + 1) * HEAD_DIM)
        q = q_ref[:, sl]
        s_c = _dot_nt(q, kc_ref[:, ksl])
        s_x = _dot_nt(q, kx_ref[:, ksl])
        m = jnp.maximum(jnp.max(s_c, axis=-1, keepdims=True), jnp.max(s_x, axis=-1, keepdims=True))
        p_c = jnp.exp2(s_c - m).astype(BF16)
        p_x = jnp.exp2(s_x - m).astype(BF16)
        o = _dot(p_c, vc_ref[:, vsl]) + _dot(p_x, vx_ref[:, vsl])
        gate = _silu(g_ref[:, sl].astype(F32))
        heads.append((gate * (o[:, :HEAD_DIM] / o[:, HEAD_DIM:])).astype(BF16))
    y = _dot(jnp.concatenate(heads, axis=1), w_ref[...])
    x = x_ref[...] + mod_ref[:, 2 * d:3 * d] * y
    ms = jnp.mean(x * x, axis=-1, keepdims=True)
    out_ref[...] = x * lax.rsqrt(ms + EPS) * fg_ref[...]


def _attention(q, kc, vc, kx, vx, g, x2, mod4, w, fg, batch, ctx_len, seq, tq):
    d = x2.shape[-1]
    nt = seq // tq
    kv = lambda rows, w_: pl.BlockSpec((rows, w_), lambda b, t: (b, 0))
    tok = lambda w_: pl.BlockSpec((tq, w_), lambda b, t: (b * nt + t, 0))
    const = lambda a: pl.BlockSpec(a.shape, lambda b, t: (0,) * a.ndim)
    return pl.pallas_call(
        _attn_kernel,
        grid=(batch, nt),
        in_specs=[tok(ATT_W), kv(ctx_len, KV_W), kv(ctx_len, 2 * KV_W), kv(seq, KV_W), kv(seq, 2 * KV_W),
                  tok(ATT_W), tok(d),
                  pl.BlockSpec((None, None, 1, 3 * d), lambda b, t: (1, b, 0, 0)), const(w), const(fg)],
        out_specs=tok(d),
        out_shape=jax.ShapeDtypeStruct((batch * seq, d), F32),
        compiler_params=_params(("arbitrary", "arbitrary")),
        name="attention",
    )(q, kc, vc, kx, vx, g, x2, mod4, w, fg)


def _rope_tables(seq):
    rows = seq // GRID_W
    r = jnp.repeat(jnp.arange(rows, dtype=F32), GRID_W)
    col = jnp.tile(jnp.arange(GRID_W, dtype=F32), rows)
    half = HEAD_DIM // 2
    inv = ROPE_THETA ** (-jnp.arange(0, half, 2, dtype=F32) / half)
    ang = jnp.concatenate([r[:, None] * inv, col[:, None] * inv], axis=-1)
    cos, sin = jnp.cos(ang), jnp.sin(ang)
    return jnp.concatenate([cos, cos], axis=-1), jnp.concatenate([-sin, sin], axis=-1)


def kernel(x, c, ctx, c_ctx, norm_g, ada_w, ada_b, er_in_w, er_out_w, hy_conv_w, hy_conv_b, hy_f_w1, hy_f_b1, hy_f_freq, hy_f_w2, hy_f_b2, hy_f_w3, hy_bias, ret_decay_f, ret_decay_b, ret_norm_g, at_in_w, at_out_w, at_q_norm, at_k_norm, final_norm_g):
    batch, seq, d = x.shape
    ctx_len = ctx.shape[1]
    assert seq % TOEP == 0 and ctx_len % TOEP == 0 and seq % GRID_W == 0 and d == 2 * HY_W
    tm = min(512, seq)

    rows = -(-(batch + 1) // 8) * 8
    cond = jnp.concatenate([c, c_ctx[None, :], jnp.zeros((rows - batch - 1, d), F32)], axis=0)
    mod4 = _ada(cond, ada_w, ada_b).reshape(ada_w.shape[0], rows, 1, 3 * d)
    lat_row = lambda b: b
    ctx_row = lambda b: batch

    x2 = x.reshape(batch * seq, d)
    c2 = ctx.reshape(batch * ctx_len, d)

    in_w = er_in_w[0].astype(BF16)
    col = lambda p: in_w[:, p * HY_W:(p + 1) * HY_W].T
    w_hy = jnp.stack([jnp.concatenate([col(0), col(3)], axis=0), jnp.concatenate([col(1), col(2)], axis=0)])
    w_ret = in_w[:, 4 * HY_W:]
    cw = hy_conv_w[0].T
    cb = hy_conv_b[0].reshape(3 * HY_W, 1)
    g0 = norm_g[0].reshape(1, d)
    hy_x, h_x = _inproj_hy(x2, mod4, 0, lat_row, g0, w_hy, cw, cb, batch, seq, seq)
    ctx_rows = math.gcd(batch * ctx_len, seq)
    hy_c, h_c = _inproj_hy(c2, mod4, 0, ctx_row, g0, w_hy, cw, cb, batch * ctx_len // ctx_rows, ctx_rows, ctx_len)
    zret_x = _inproj_ret(h_x, w_ret, min(2 * tm, h_x.shape[0]))
    zret_c = _inproj_ret(h_c, w_ret, min(2 * tm, h_c.shape[0]))

    filt = (hy_f_w1[0], hy_f_b1[0], hy_f_freq[0], hy_f_w2[0], hy_f_b2[0], hy_f_w3[0], hy_bias[0])
    yhyt_x, yhyt_c = _hyena((hy_x, hy_c), (_filters(seq, *filt), _filters(ctx_len, *filt)), batch, (seq, ctx_len))
    yret_c, yret_x = _retention(zret_c, zret_x, ret_decay_f[0], ret_decay_b[0], ret_norm_g[0], batch, ctx_len, seq)

    wo_hy = er_out_w[0][:HY_W].astype(BF16)
    wo_ret = er_out_w[0][HY_W:].astype(BF16)
    perm = jnp.concatenate([jnp.arange(0, HEAD_DIM, 2), jnp.arange(1, HEAD_DIM, 2)])
    qk_cols = (jnp.arange(N_HEADS + N_KV)[:, None] * HEAD_DIM + perm[None, :]).reshape(-1)
    w1 = at_in_w[0]
    win_x = jnp.concatenate([w1[:, qk_cols], w1[:, ATT_W + KV_W:]], axis=1).astype(BF16)
    win_c = win_x[:, ATT_W:ATT_W + 2 * KV_W]
    qn = at_q_norm[0][perm].reshape(1, HEAD_DIM)
    kn = at_k_norm[0][perm].reshape(1, HEAD_DIM)
    g1 = norm_g[1].reshape(1, d)
    cosf, sinf = _rope_tables(seq)
    x1, q, kx, vx, gate = _mid(x2, yhyt_x.reshape(HY_W, batch * seq), yret_x, mod4, lat_row, wo_hy, wo_ret, g1,
                               win_x, qn, kn, cosf, sinf, batch, seq, tm, True)
    kc, vc = _mid(c2, yhyt_c.reshape(HY_W, batch * ctx_len), yret_c, mod4, ctx_row, wo_hy, wo_ret, g1,
                  win_c, qn, kn, None, None, batch, ctx_len, ctx_len, False)

    out = _attention(q, kc, vc, kx, vx, gate, x1, mod4, at_out_w[0].astype(BF16), final_norm_g.reshape(1, d),
                     batch, ctx_len, seq, min(1024, seq))
    return out.reshape(batch, seq, d)
```

```python
import functools
import math

import jax
import jax.numpy as jnp
from jax import lax
from jax.experimental import pallas as pl
from jax.experimental.pallas import tpu as pltpu

F32 = jnp.float32
BF16 = jnp.bfloat16
HIGHEST = lax.Precision.HIGHEST

EPS = 1e-6
GRID_W = 64
HY_W = 512
HY_EMB = 33
HY_BANDS = (HY_EMB - 1) // 2
HY_ORDER = 64
HY_DECAY_TARGET = 1e-2
HY_FAST_PCT = 0.3
HY_SLOW_PCT = 1.5
RET_W = 512
RET_H = 4
RET_HD = RET_W // RET_H
CHUNK = 128
N_HEADS = 8
N_KV = 2
HEAD_DIM = 128
GROUP = N_HEADS // N_KV
ATT_W = N_HEADS * HEAD_DIM
KV_W = N_KV * HEAD_DIM
ROPE_THETA = 10000.0
Q_PRESCALE = HEAD_DIM ** -0.5 * math.log2(math.e)

MID_ROWS = 256
HY_CH = 256
LANES = 128
TOEP = 256
VMEM_LIMIT = 56 * 1024 * 1024
T_DEC, T_QDF, T_QDB, T_KDF, T_KDB, T_CDF, T_CDB = range(7)
N_TABLES = 7


def _params(sem):
    return pltpu.CompilerParams(dimension_semantics=sem, vmem_limit_bytes=VMEM_LIMIT)


def _silu(x):
    return x * (1.0 / (1.0 + jnp.exp(-x)))


def _dot(a, b):
    return jnp.dot(a, b, preferred_element_type=F32)


def _dot_nt(a, b):
    return lax.dot_general(a, b, (((1,), (1,)), ((), ())), preferred_element_type=F32)


def _dot_tn(a, b):
    return lax.dot_general(a, b, (((0,), (0,)), ((), ())), preferred_element_type=F32)


def _mod_norm(x, g, shift, scale):
    ms = jnp.mean(x * x, axis=-1, keepdims=True)
    return (x * lax.rsqrt(ms + EPS) * g) * (1.0 + scale) + shift


def _ada_kernel(cond_ref, w_ref, b_ref, o_ref):
    s = _silu(cond_ref[...])
    o_ref[...] = jnp.dot(s, w_ref[...], preferred_element_type=F32, precision=HIGHEST) + b_ref[...]


def _ada(cond, ada_w, ada_b):
    depth, d, n = ada_w.shape
    rows = cond.shape[0]
    tn = 1024
    return pl.pallas_call(
        _ada_kernel,
        grid=(depth, n // tn),
        in_specs=[
            pl.BlockSpec((rows, d), lambda i, j: (0, 0)),
            pl.BlockSpec((None, d, tn), lambda i, j: (i, 0, j)),
            pl.BlockSpec((None, 1, tn), lambda i, j: (i, 0, j)),
        ],
        out_specs=pl.BlockSpec((None, rows, tn), lambda i, j: (i, 0, j)),
        out_shape=jax.ShapeDtypeStruct((depth, rows, n), F32),
        compiler_params=_params(("arbitrary", "arbitrary")),
        name="ada",
    )(cond, ada_w, ada_b.reshape(depth, 1, n))


def _inproj_hy_kernel(x_ref, mod_ref, g_ref, w_ref, cw_ref, cb_ref, hy_ref, h_ref, *, period):
    seq, d = x_ref.shape
    rows = min(seq, 2 * MID_ROWS)
    for r in range(0, seq, rows):
        rs = slice(r, r + rows)
        h_ref[rs, :] = _mod_norm(x_ref[rs, :], g_ref[...], mod_ref[:, 0:d], mod_ref[:, d:2 * d]).astype(BF16)

    lane = lax.broadcasted_iota(jnp.int32, (1, seq), 1)
    first = lane % period == 0
    last = lane % period == period - 1

    def conv(a, part, c0):
        ch = slice(part * HY_W + c0, part * HY_W + c0 + a.shape[0])
        left = jnp.where(first, 0.0, pltpu.roll(a, 1, 1))
        right = jnp.where(last, 0.0, pltpu.roll(a, seq - 1, 1))
        return left * cw_ref[ch, 0:1] + a * cw_ref[ch, 1:2] + right * cw_ref[ch, 2:3] + cb_ref[ch, :]

    def hyena_part(p, combine):
        h = h_ref[...]
        sub = 8
        for c0 in range(0, HY_W, HY_CH):
            za = _dot_nt(w_ref[p, c0:c0 + HY_CH, :], h)
            zb = _dot_nt(w_ref[p, HY_W + c0:HY_W + c0 + HY_CH, :], h)
            for r in range(0, HY_CH, 2 * sub):
                pieces = [combine(za[r + i:r + i + sub, :], zb[r + i:r + i + sub, :], c0 + r + i) for i in (0, sub)]
                hy_ref[p, c0 + r:c0 + r + 2 * sub, :] = jnp.concatenate(pieces, axis=0).astype(BF16)

    hyena_part(0, lambda x0, gate, c0: conv(x0, 0, c0) * _silu(gate))
    hyena_part(1, lambda x1, v, c0: conv(x1, 1, c0) * conv(v, 2, c0))


def _inproj_hy(x2, mod4, layer, mod_row, g, w_hy, cw, cb, batch, seq, period):
    d = x2.shape[-1]
    return pl.pallas_call(
        functools.partial(_inproj_hy_kernel, period=period),
        grid=(batch,),
        in_specs=[
            pl.BlockSpec((seq, d), lambda b: (b, 0)),
            pl.BlockSpec((None, None, 1, 3 * d), lambda b: (layer, mod_row(b), 0, 0)),
            pl.BlockSpec((1, d), lambda b: (0, 0)),
            pl.BlockSpec(w_hy.shape, lambda b: (0, 0, 0), pipeline_mode=pl.Buffered(1)),
            pl.BlockSpec(cw.shape, lambda b: (0, 0), pipeline_mode=pl.Buffered(1)),
            pl.BlockSpec(cb.shape, lambda b: (0, 0), pipeline_mode=pl.Buffered(1)),
        ],
        out_specs=[
            pl.BlockSpec((2, HY_W, seq), lambda b: (0, 0, b)),
            pl.BlockSpec((seq, d), lambda b: (b, 0)),
        ],
        out_shape=[
            jax.ShapeDtypeStruct((2, HY_W, batch * seq), BF16),
            jax.ShapeDtypeStruct((batch * seq, d), BF16),
        ],
        compiler_params=_params(("arbitrary",)),
        name="inproj_hy",
    )(x2, mod4, g, w_hy, cw, cb)


def _inproj_ret_kernel(h_ref, w_ref, z_ref):
    n = w_ref.shape[1]
    step = 2 * RET_W
    for c in range(0, n, step):
        z_ref[:, c:c + step] = _dot(h_ref[...], w_ref[:, c:c + step]).astype(BF16)


def _inproj_ret(h, w_ret, tm):
    n, d = h.shape
    nr = w_ret.shape[1]
    return pl.pallas_call(
        _inproj_ret_kernel,
        grid=(n // tm,),
        in_specs=[pl.BlockSpec((tm, d), lambda i: (i, 0)), pl.BlockSpec((d, nr), lambda i: (0, 0))],
        out_specs=pl.BlockSpec((tm, nr), lambda i: (i, 0)),
        out_shape=jax.ShapeDtypeStruct((n, nr), BF16),
        compiler_params=_params(("arbitrary",)),
        name="inproj_ret",
    )(h, w_ret)


def _filter_kernel(w1t_ref, w1c_ref, w1s_ref, b1_ref, f_ref, w2_ref, b2_ref, w3_ref, bias_ref, o_ref, *, seq):
    two_l = o_ref.shape[-1]
    cb = o_ref.shape[0]
    lag = lax.broadcasted_iota(jnp.int32, (1, two_l), 1) - seq
    n = jnp.abs(lag).astype(F32)
    t = n / float(max(seq - 1, 1))
    band_i = lax.broadcasted_iota(jnp.int32, (HY_BANDS, 1), 0).astype(F32)
    bands = 1e-4 + band_i * ((HY_BANDS - 1 - 1e-4) / (HY_BANDS - 1))
    ang = (2.0 * math.pi / seq) * n * bands
    f = f_ref[...]
    pre = (w1t_ref[...] * t
           + jnp.dot(w1c_ref[...], jnp.cos(ang), preferred_element_type=F32, precision=HIGHEST)
           + jnp.dot(w1s_ref[...], -jnp.sin(ang), preferred_element_type=F32, precision=HIGHEST))
    h = jnp.sin(f * (pre + b1_ref[...]))
    h = jnp.sin(f * (jnp.dot(w2_ref[...], h, preferred_element_type=F32, precision=HIGHEST) + b2_ref[...]))
    h_b = jnp.dot(w3_ref[1], h[:, :seq], preferred_element_type=F32, precision=HIGHEST)
    h_f = jnp.dot(w3_ref[0], h[:, seq:], preferred_element_type=F32, precision=HIGHEST)
    max_decay = math.log(HY_DECAY_TARGET) / HY_FAST_PCT
    min_decay = math.log(HY_DECAY_TARGET) / HY_SLOW_PCT
    ch = (lax.broadcasted_iota(jnp.int32, (cb, 1), 0) + pl.program_id(0) * cb).astype(F32)
    deltas = jnp.abs(min_decay + ch * ((max_decay - min_decay) / (HY_W - 1)))
    window = jnp.exp(-t * deltas)
    k = jnp.concatenate([h_b, h_f], axis=1) * window
    k = jnp.where(lag == -seq, 0.0, k)
    o_ref[...] = k + jnp.where(lag == 0, bias_ref[...], 0.0)


def _filters(seq, w1, b1, freq, w2, b2, w3, bias):
    cb = 128
    w1t = w1.T
    args = (
        w1t[:, 0:1], w1t[:, 1:1 + HY_BANDS], w1t[:, 1 + HY_BANDS:],
        b1.reshape(HY_ORDER, 1), freq.reshape(HY_ORDER, 1), w2.T, b2.reshape(HY_ORDER, 1),
        w3.T.reshape(2, HY_W, HY_ORDER), bias.reshape(HY_W, 1),
    )
    full = lambda a: pl.BlockSpec(a.shape, lambda i: (0,) * a.ndim)
    in_specs = [full(a) for a in args[:7]] + [
        pl.BlockSpec((2, cb, HY_ORDER), lambda i: (0, i, 0)),
        pl.BlockSpec((cb, 1), lambda i: (i, 0)),
    ]
    return pl.pallas_call(
        functools.partial(_filter_kernel, seq=seq),
        grid=(HY_W // cb,),
        in_specs=in_specs,
        out_specs=pl.BlockSpec((cb, 2 * seq), lambda i: (i, 0)),
        out_shape=jax.ShapeDtypeStruct((HY_W, 2 * seq), F32),
        compiler_params=_params(("arbitrary",)),
        name="hyena_filters",
    )(*args)


def _hyena_kernel(*refs, seqs, batch):
    ns = len(seqs)
    ins, outs, scr = refs[:3 * ns], refs[3 * ns:4 * ns], refs[4 * ns:]
    cblk = outs[0].shape[0]
    row_i = lax.broadcasted_iota(jnp.int32, (LANES, LANES), 0)
    col_i = lax.broadcasted_iota(jnp.int32, (LANES, LANES), 1)
    upper = col_i >= row_i

    def channel(c, seq, xg_ref, u_ref, kext_ref, out_ref, t_ref, u2_ref, y2_ref):
        nb = seq // TOEP
        half = seq // LANES
        krow = kext_ref[c:c + 1, :]

        def rotated(q):
            a = jnp.broadcast_to(krow[:, q * LANES:(q + 1) * LANES], (LANES, LANES))
            return pltpu.roll(a, 0, 1, stride=1, stride_axis=0)

        e_lo = -(2 * nb - 1)
        r_prev = rotated(half + e_lo - 1)
        for e in range(e_lo, 2 * nb):
            r_cur = rotated(half + e)
            g = jnp.where(upper, r_cur, r_prev).astype(BF16)
            r_prev = r_cur
            if e % 2 == 0:
                d = e // 2
                t_ref[d + nb - 1, 0:LANES, 0:LANES] = g
                t_ref[d + nb - 1, LANES:TOEP, LANES:TOEP] = g
            else:
                d = (e - 1) // 2
                if abs(d) <= nb - 1:
                    t_ref[d + nb - 1, 0:LANES, LANES:TOEP] = g
                d = (e + 1) // 2
                if abs(d) <= nb - 1:
                    t_ref[d + nb - 1, LANES:TOEP, 0:LANES] = g
        for j in range(nb):
            u2_ref[j * batch:(j + 1) * batch, :] = u_ref[c, :, j * TOEP:(j + 1) * TOEP]
        y2_ref[...] = _dot(u2_ref[...], t_ref[nb - 1])
        for d in list(range(1, nb)) + list(range(-(nb - 1), 0)):
            j_lo, j_hi = max(0, -d), nb - max(0, d)
            res = _dot(u2_ref[j_lo * batch:j_hi * batch, :], t_ref[d + nb - 1])
            y2_ref[(j_lo + d) * batch:(j_hi + d) * batch, :] += res
        for j in range(nb):
            sl = slice(j * TOEP, (j + 1) * TOEP)
            out_ref[c, :, sl] = (xg_ref[c, :, sl].astype(F32) * y2_ref[j * batch:(j + 1) * batch, :]).astype(BF16)

    for c in range(cblk):
        for s, seq in enumerate(seqs):
            t_all, u2_all, y2_all = scr[3 * s:3 * s + 3]
            channel(c, seq, *ins[3 * s:3 * s + 3], outs[s], t_all.at[c], u2_all.at[c], y2_all.at[c])


def _hyena(hys, kexts, batch, seqs):
    cblk = 8
    args, in_specs, out_specs, out_shape, scratch = [], [], [], [], []
    for hy, kext, seq in zip(hys, kexts, seqs):
        nb = seq // TOEP
        hy4 = hy.reshape(2, HY_W, batch, seq)
        part = lambda p, seq=seq: pl.BlockSpec((None, cblk, batch, seq), lambda i: (p, i, 0, 0))
        args += [hy4, hy4, kext]
        in_specs += [part(0), part(1), pl.BlockSpec((cblk, 2 * seq), lambda i: (i, 0))]
        out_specs.append(pl.BlockSpec((cblk, batch, seq), lambda i: (i, 0, 0)))
        out_shape.append(jax.ShapeDtypeStruct((HY_W, batch, seq), BF16))
        scratch += [pltpu.VMEM((cblk, 2 * nb - 1, TOEP, TOEP), BF16),
                    pltpu.VMEM((cblk, nb * batch, TOEP), BF16),
                    pltpu.VMEM((cblk, nb * batch, TOEP), F32)]
    return pl.pallas_call(
        functools.partial(_hyena_kernel, seqs=tuple(seqs), batch=batch),
        grid=(HY_W // cblk,),
        in_specs=in_specs,
        out_specs=out_specs,
        out_shape=out_shape,
        scratch_shapes=scratch,
        compiler_params=_params(("arbitrary",)),
        name="hyena",
    )(*args)


def _retention_tables_kernel(af_ref, ab_ref, tab_ref):
    k_scale = RET_HD ** -0.5
    ri = lax.broadcasted_iota(jnp.int32, (CHUNK, CHUNK), 0).astype(F32)
    ci = lax.broadcasted_iota(jnp.int32, (CHUNK, CHUNK), 1).astype(F32)
    diff = ri - ci
    for h in range(RET_H):
        log_f = -jnp.exp(jnp.full((CHUNK, CHUNK), af_ref[h], F32))
        log_b = -jnp.exp(jnp.full((CHUNK, CHUNK), ab_ref[h], F32))
        dec_f = jnp.where(diff >= 0, jnp.exp(log_f * jnp.maximum(diff, 0.0)), 0.0)
        dec_b = jnp.where(diff <= 0, jnp.exp(log_b * jnp.maximum(-diff, 0.0)), 0.0)
        tab_ref[h, T_DEC] = (dec_f + dec_b) * k_scale
        tab_ref[h, T_QDF] = jnp.exp(log_f * (ri + 1.0))
        tab_ref[h, T_QDB] = jnp.exp(log_b * (CHUNK - ri))
        tab_ref[h, T_KDF] = jnp.exp(log_f * (CHUNK - 1.0 - ri)) * k_scale
        tab_ref[h, T_KDB] = jnp.exp(log_b * ri) * k_scale
        tab_ref[h, T_CDF] = jnp.exp(log_f * CHUNK)
        tab_ref[h, T_CDB] = jnp.exp(log_b * CHUNK)


def _retention_tables(a_f, a_b):
    smem = pl.BlockSpec(memory_space=pltpu.SMEM)
    return pl.pallas_call(
        _retention_tables_kernel,
        in_specs=[smem, smem],
        out_shape=jax.ShapeDtypeStruct((RET_H, N_TABLES, CHUNK, CHUNK), F32),
        name="retention_tables",
    )(a_f, a_b)


def _retention_kernel(tab_ref, zc_ref, zx_ref, ng_ref, oc_ref, ox_ref, sf_ref, sb_ref, sbc_ref, sbx_ref):
    hd = RET_HD
    sf_ref[...] = jnp.zeros_like(sf_ref)
    sb_ref[...] = jnp.zeros_like(sb_ref)

    def head_cols(part, h):
        return slice((part * RET_H + h) * hd, (part * RET_H + h + 1) * hd)

    def state_update(s, k, v, kd, cd):
        kdt = jnp.transpose(k.astype(F32) * kd).astype(BF16)
        return s * cd + _dot(kdt, v)

    def state_pass(z_ref, store_ref):
        nchunk = z_ref.shape[0] // CHUNK

        for i in range(nchunk):
            for t, s_ref, lo, t_kd, t_cd in ((i, sf_ref, 0, T_KDF, T_CDF),
                                             (nchunk - 1 - i, sb_ref, hd, T_KDB, T_CDB)):
                rows = slice(t * CHUNK, (t + 1) * CHUNK)
                for h in range(RET_H):
                    k = z_ref[rows, head_cols(1, h)]
                    v = z_ref[rows, head_cols(2, h)]
                    s = s_ref[h]
                    store_ref[t, h, :, lo:lo + hd] = s.astype(BF16)
                    s_ref[h] = state_update(s, k, v, tab_ref[h, t_kd], tab_ref[h, t_cd])

    def out_pass(z_ref, store_ref, out_ref):
        nchunk = z_ref.shape[0] // CHUNK

        def first_dots(t, h):
            rows = slice(t * CHUNK, (t + 1) * CHUNK)
            q = z_ref[rows, head_cols(0, h)]
            k = z_ref[rows, head_cols(1, h)]
            return _dot_nt(q, k), _dot(q, store_ref[t, h])

        def finish(t, h, scores, cross):
            rows = slice(t * CHUNK, (t + 1) * CHUNK)
            v = z_ref[rows, head_cols(2, h)]
            o = (_dot((scores * tab_ref[h, T_DEC]).astype(BF16), v) + cross[:, :hd] * tab_ref[h, T_QDF]
                 + cross[:, hd:] * tab_ref[h, T_QDB])
            mu = jnp.mean(o, axis=-1, keepdims=True)
            oc = o - mu
            var = jnp.mean(oc * oc, axis=-1, keepdims=True)
            y = oc * lax.rsqrt(var + EPS) * ng_ref[:, h * hd:(h + 1) * hd]
            gate = _silu(z_ref[rows, head_cols(3, h)].astype(F32))
            out_ref[rows, h * hd:(h + 1) * hd] = (y * gate).astype(BF16)

        items = [(t, h) for t in range(nchunk) for h in range(RET_H)]
        dots = first_dots(*items[0])
        for i, item in enumerate(items):
            nxt = first_dots(*items[i + 1]) if i + 1 < len(items) else None
            finish(*item, *dots)
            dots = nxt

    state_pass(zc_ref, sbc_ref)
    state_pass(zx_ref, sbx_ref)
    out_pass(zc_ref, sbc_ref, oc_ref)
    out_pass(zx_ref, sbx_ref, ox_ref)


def _retention(zret_c, zret_x, a_f, a_b, norm_g, batch, ctx_len, seq):
    hd = RET_HD
    tables = _retention_tables(a_f, a_b)
    return pl.pallas_call(
        _retention_kernel,
        grid=(batch,),
        in_specs=[pl.BlockSpec(tables.shape, lambda b: (0, 0, 0, 0)),
                  pl.BlockSpec((ctx_len, 4 * RET_W), lambda b: (b, 0)),
                  pl.BlockSpec((seq, 4 * RET_W), lambda b: (b, 0)),
                  pl.BlockSpec((1, RET_W), lambda b: (0, 0))],
        out_specs=[pl.BlockSpec((ctx_len, RET_W), lambda b: (b, 0)),
                   pl.BlockSpec((seq, RET_W), lambda b: (b, 0))],
        out_shape=[jax.ShapeDtypeStruct((batch * ctx_len, RET_W), BF16),
                   jax.ShapeDtypeStruct((batch * seq, RET_W), BF16)],
        scratch_shapes=[
            pltpu.VMEM((RET_H, hd, hd), F32),
            pltpu.VMEM((RET_H, hd, hd), F32),
            pltpu.VMEM((ctx_len // CHUNK, RET_H, hd, 2 * hd), BF16),
            pltpu.VMEM((seq // CHUNK, RET_H, hd, 2 * hd), BF16),
        ],
        compiler_params=_params(("arbitrary",)),
        name="retention",
    )(tables, zret_c, zret_x, norm_g.reshape(1, RET_W))


def _head_norm_rope(z, gain, cos, sin):
    ms = jnp.mean(z * z, axis=-1, keepdims=True)
    y = z * lax.rsqrt(ms + EPS) * gain
    if cos is None:
        return y
    return y * cos + pltpu.roll(y, HEAD_DIM // 2, 1) * sin


def _store_v_aug(v_ref, v):
    ones = jnp.ones((v.shape[0], HEAD_DIM), BF16)
    for hd in range(N_KV):
        v_ref[:, 2 * hd * HEAD_DIM:(2 * hd + 1) * HEAD_DIM] = v[:, hd * HEAD_DIM:(hd + 1) * HEAD_DIM].astype(BF16)
        v_ref[:, (2 * hd + 1) * HEAD_DIM:(2 * hd + 2) * HEAD_DIM] = ones


def _mid_kernel(x_ref, yhyt_ref, yret_ref, mod0_ref, mod1_ref, wo_hy_ref, wo_ret_ref, g1_ref, win_ref,
                qn_ref, kn_ref, *rest, latent):
    d = x_ref.shape[-1]
    if latent:
        cos_ref, sin_ref, xo_ref, q_ref, k_ref, v_ref, g_ref = rest
    else:
        k_ref, v_ref = rest
    tm = x_ref.shape[0]
    sub = min(tm, MID_ROWS)
    pair = 2 * HEAD_DIM

    def front(rs):
        y = _dot_tn(yhyt_ref[:, rs], wo_hy_ref[...]) + _dot(yret_ref[rs, :], wo_ret_ref[...])
        x = x_ref[rs, :] + mod0_ref[:, 2 * d:3 * d] * y
        if latent:
            xo_ref[rs, :] = x
        return _mod_norm(x, g1_ref[...], mod1_ref[:, 0:d], mod1_ref[:, d:2 * d]).astype(BF16)

    def back(rs, h):
        cos, sin = (cos_ref[rs, :], sin_ref[rs, :]) if latent else (None, None)
        pieces = []

        def q_pair(off):
            z2 = _dot(h, win_ref[:, off:off + pair])
            for i in range(2):
                q = _head_norm_rope(z2[:, i * HEAD_DIM:(i + 1) * HEAD_DIM], qn_ref[...], cos, sin) * Q_PRESCALE
                q_ref[rs, off + i * HEAD_DIM:off + (i + 1) * HEAD_DIM] = q.astype(BF16)

        def k_pair(off):
            z2 = _dot(h, win_ref[:, off:off + pair])
            for i in range(N_KV):
                k = _head_norm_rope(z2[:, i * HEAD_DIM:(i + 1) * HEAD_DIM], kn_ref[...], cos, sin)
                k_ref[rs, i * HEAD_DIM:(i + 1) * HEAD_DIM] = k.astype(BF16)

        def v_part(off):
            _store_v_aug(v_ref.at[rs, :], _dot(h, win_ref[:, off:off + KV_W]))

        def g_part(off, c):
            g_ref[rs, c:c + pair] = _dot(h, win_ref[:, off + c:off + c + pair]).astype(BF16)

        off = 0
        if latent:
            for hp in range(N_HEADS // 2):
                pieces.append(functools.partial(q_pair, off))
                off += pair
        pieces.append(functools.partial(k_pair, off))
        off += pair
        pieces.append(functools.partial(v_part, off))
        off += KV_W
        if latent:
            for c in range(0, ATT_W, pair):
                pieces.append(functools.partial(g_part, off, c))
        return pieces

    groups = [slice(r, r + sub) for r in range(0, tm, sub)]
    h = front(groups[0])
    for gi, rs in enumerate(groups):
        pieces = back(rs, h)
        cut = len(pieces) // 4
        for piece in pieces[:cut]:
            piece()
        if gi + 1 < len(groups):
            h = front(groups[gi + 1])
        for piece in pieces[cut:]:
            piece()


def _mid(x2, yhyt, yret, mod4, mod_row, wo_hy, wo_ret, g1, win, qn, kn, cosf, sinf, batch, seq, tm, latent):
    d = x2.shape[-1]
    nt = seq // tm
    tok = lambda w: pl.BlockSpec((tm, w), lambda b, t: (b * nt + t, 0))
    const = lambda a: pl.BlockSpec(a.shape, lambda b, t: (0,) * a.ndim)
    modspec = lambda layer: pl.BlockSpec((None, None, 1, 3 * d), lambda b, t: (layer, mod_row(b), 0, 0))
    in_specs = [tok(d), pl.BlockSpec((HY_W, tm), lambda b, t: (0, b * nt + t)), tok(RET_W),
                modspec(0), modspec(1), const(wo_hy), const(wo_ret), const(g1), const(win), const(qn), const(kn)]
    args = [x2, yhyt, yret, mod4, mod4, wo_hy, wo_ret, g1, win, qn, kn]
    n = batch * seq
    if latent:
        in_specs += [pl.BlockSpec((tm, HEAD_DIM), lambda b, t: (t, 0))] * 2
        args += [cosf, sinf]
        out_specs = [tok(d), tok(ATT_W), tok(KV_W), tok(2 * KV_W), tok(ATT_W)]
        out_shape = [jax.ShapeDtypeStruct((n, d), F32), jax.ShapeDtypeStruct((n, ATT_W), BF16),
                     jax.ShapeDtypeStruct((n, KV_W), BF16), jax.ShapeDtypeStruct((n, 2 * KV_W), BF16),
                     jax.ShapeDtypeStruct((n, ATT_W), BF16)]
    else:
        out_specs = [tok(KV_W), tok(2 * KV_W)]
        out_shape = [jax.ShapeDtypeStruct((n, KV_W), BF16), jax.ShapeDtypeStruct((n, 2 * KV_W), BF16)]
    return pl.pallas_call(
        functools.partial(_mid_kernel, latent=latent),
        grid=(batch, nt),
        in_specs=in_specs,
        out_specs=out_specs,
        out_shape=out_shape,
        compiler_params=_params(("arbitrary", "arbitrary")),
        name="mid_latent" if latent else "mid_ctx",
    )(*args)


def _attn_kernel(q_ref, kc_ref, vc_ref, kx_ref, vx_ref, g_ref, x_ref, mod_ref, w_ref, fg_ref, out_ref):
    d = x_ref.shape[-1]
    heads = []
    for hd in range(N_HEADS):
        kvh = hd // GROUP
        ksl = slice(kvh * HEAD_DIM, (kvh + 1) * HEAD_DIM)
        vsl = slice(2 * kvh * HEAD_DIM, 2 * (kvh + 1) * HEAD_DIM)
        sl = slice(hd * HEAD_DIM, (hd + 1) * HEAD_DIM)
        q = q_ref[:, sl]
        s_c = _dot_nt(q, kc_ref[:, ksl])
        s_x = _dot_nt(q, kx_ref[:, ksl])
        m = jnp.maximum(jnp.max(s_c, axis=-1, keepdims=True), jnp.max(s_x, axis=-1, keepdims=True))
        p_c = jnp.exp2(s_c - m).astype(BF16)
        p_x = jnp.exp2(s_x - m).astype(BF16)
        o = _dot(p_c, vc_ref[:, vsl]) + _dot(p_x, vx_ref[:, vsl])
        gate = _silu(g_ref[:, sl].astype(F32))
        heads.append((gate * (o[:, :HEAD_DIM] / o[:, HEAD_DIM:])).astype(BF16))
    y = _dot(jnp.concatenate(heads, axis=1), w_ref[...])
    x = x_ref[...] + mod_ref[:, 2 * d:3 * d] * y
    ms = jnp.mean(x * x, axis=-1, keepdims=True)
    out_ref[...] = x * lax.rsqrt(ms + EPS) * fg_ref[...]


def _attention(q, kc, vc, kx, vx, g, x2, mod4, w, fg, batch, ctx_len, seq, tq):
    d = x2.shape[-1]
    nt = seq // tq
    kv = lambda rows, w_: pl.BlockSpec((rows, w_), lambda b, t: (b, 0))
    tok = lambda w_: pl.BlockSpec((tq, w_), lambda b, t: (b * nt + t, 0))
    const = lambda a: pl.BlockSpec(a.shape, lambda b, t: (0,) * a.ndim)
    return pl.pallas_call(
        _attn_kernel,
        grid=(batch, nt),
        in_specs=[tok(ATT_W), kv(ctx_len, KV_W), kv(ctx_len, 2 * KV_W), kv(seq, KV_W), kv(seq, 2 * KV_W),
                  tok(ATT_W), tok(d),
                  pl.BlockSpec((None, None, 1, 3 * d), lambda b, t: (1, b, 0, 0)), const(w), const(fg)],
        out_specs=tok(d),
        out_shape=jax.ShapeDtypeStruct((batch * seq, d), F32),
        compiler_params=_params(("arbitrary", "arbitrary")),
        name="attention",
    )(q, kc, vc, kx, vx, g, x2, mod4, w, fg)


def _rope_tables(seq):
    rows = seq // GRID_W
    r = jnp.repeat(jnp.arange(rows, dtype=F32), GRID_W)
    col = jnp.tile(jnp.arange(GRID_W, dtype=F32), rows)
    half = HEAD_DIM // 2
    inv = ROPE_THETA ** (-jnp.arange(0, half, 2, dtype=F32) / half)
    ang = jnp.concatenate([r[:, None] * inv, col[:, None] * inv], axis=-1)
    cos, sin = jnp.cos(ang), jnp.sin(ang)
    return jnp.concatenate([cos, cos], axis=-1), jnp.concatenate([-sin, sin], axis=-1)


def kernel(x, c, ctx, c_ctx, norm_g, ada_w, ada_b, er_in_w, er_out_w, hy_conv_w, hy_conv_b, hy_f_w1, hy_f_b1, hy_f_freq, hy_f_w2, hy_f_b2, hy_f_w3, hy_bias, ret_decay_f, ret_decay_b, ret_norm_g, at_in_w, at_out_w, at_q_norm, at_k_norm, final_norm_g):
    batch, seq, d = x.shape
    ctx_len = ctx.shape[1]
    assert seq % TOEP == 0 and ctx_len % TOEP == 0 and seq % GRID_W == 0 and d == 2 * HY_W
    tm = min(512, seq)

    rows = -(-(batch + 1) // 8) * 8
    cond = jnp.concatenate([c, c_ctx[None, :], jnp.zeros((rows - batch - 1, d), F32)], axis=0)
    mod4 = _ada(cond, ada_w, ada_b).reshape(ada_w.shape[0], rows, 1, 3 * d)
    lat_row = lambda b: b
    ctx_row = lambda b: batch

    x2 = x.reshape(batch * seq, d)
    c2 = ctx.reshape(batch * ctx_len, d)

    in_w = er_in_w[0].astype(BF16)
    col = lambda p: in_w[:, p * HY_W:(p + 1) * HY_W].T
    w_hy = jnp.stack([jnp.concatenate([col(0), col(3)], axis=0), jnp.concatenate([col(1), col(2)], axis=0)])
    w_ret = in_w[:, 4 * HY_W:]
    cw = hy_conv_w[0].T
    cb = hy_conv_b[0].reshape(3 * HY_W, 1)
    g0 = norm_g[0].reshape(1, d)
    hy_x, h_x = _inproj_hy(x2, mod4, 0, lat_row, g0, w_hy, cw, cb, batch, seq, seq)
    ctx_rows = math.gcd(batch * ctx_len, seq)
    hy_c, h_c = _inproj_hy(c2, mod4, 0, ctx_row, g0, w_hy, cw, cb, batch * ctx_len // ctx_rows, ctx_rows, ctx_len)
    zret_x = _inproj_ret(h_x, w_ret, min(2 * tm, h_x.shape[0]))
    zret_c = _inproj_ret(h_c, w_ret, min(2 * tm, h_c.shape[0]))

    filt = (hy_f_w1[0], hy_f_b1[0], hy_f_freq[0], hy_f_w2[0], hy_f_b2[0], hy_f_w3[0], hy_bias[0])
    yhyt_x, yhyt_c = _hyena((hy_x, hy_c), (_filters(seq, *filt), _filters(ctx_len, *filt)), batch, (seq, ctx_len))
    yret_c, yret_x = _retention(zret_c, zret_x, ret_decay_f[0], ret_decay_b[0], ret_norm_g[0], batch, ctx_len, seq)

    wo_hy = er_out_w[0][:HY_W].astype(BF16)
    wo_ret = er_out_w[0][HY_W:].astype(BF16)
    perm = jnp.concatenate([jnp.arange(0, HEAD_DIM, 2), jnp.arange(1, HEAD_DIM, 2)])
    qk_cols = (jnp.arange(N_HEADS + N_KV)[:, None] * HEAD_DIM + perm[None, :]).reshape(-1)
    w1 = at_in_w[0]
    win_x = jnp.concatenate([w1[:, qk_cols], w1[:, ATT_W + KV_W:]], axis=1).astype(BF16)
    win_c = win_x[:, ATT_W:ATT_W + 2 * KV_W]
    qn = at_q_norm[0][perm].reshape(1, HEAD_DIM)
    kn = at_k_norm[0][perm].reshape(1, HEAD_DIM)
    g1 = norm_g[1].reshape(1, d)
    cosf, sinf = _rope_tables(seq)
    x1, q, kx, vx, gate = _mid(x2, yhyt_x.reshape(HY_W, batch * seq), yret_x, mod4, lat_row, wo_hy, wo_ret, g1,
                               win_x, qn, kn, cosf, sinf, batch, seq, tm, True)
    kc, vc = _mid(c2, yhyt_c.reshape(HY_W, batch * ctx_len), yret_c, mod4, ctx_row, wo_hy, wo_ret, g1,
                  win_c, qn, kn, None, None, batch, ctx_len, ctx_len, False)

    out = _attention(q, kc, vc, kx, vx, gate, x1, mod4, at_out_w[0].astype(BF16), final_norm_g.reshape(1, d),
                     batch, ctx_len, seq, min(1024, seq))
    return out.reshape(batch, seq, d)
```

```python
import functools
import math

import jax
import jax.numpy as jnp
from jax import lax
from jax.experimental import pallas as pl
from jax.experimental.pallas import tpu as pltpu

F32 = jnp.float32
BF16 = jnp.bfloat16
HIGHEST = lax.Precision.HIGHEST

EPS = 1e-6
GRID_W = 64
HY_W = 512
HY_EMB = 33
HY_BANDS = (HY_EMB - 1) // 2
HY_ORDER = 64
HY_DECAY_TARGET = 1e-2
HY_FAST_PCT = 0.3
HY_SLOW_PCT = 1.5
RET_W = 512
RET_H = 4
RET_HD = RET_W // RET_H
CHUNK = 128
N_HEADS = 8
N_KV = 2
HEAD_DIM = 128
GROUP = N_HEADS // N_KV
ATT_W = N_HEADS * HEAD_DIM
KV_W = N_KV * HEAD_DIM
ROPE_THETA = 10000.0
Q_PRESCALE = HEAD_DIM ** -0.5 * math.log2(math.e)

MID_ROWS = 256
HY_CH = 256
LANES = 128
TOEP = 256
VMEM_LIMIT = 56 * 1024 * 1024
T_DEC, T_QDF, T_QDB, T_KDF, T_KDB, T_CDF, T_CDB = range(7)
N_TABLES = 7


def _params(sem):
    return pltpu.CompilerParams(dimension_semantics=sem, vmem_limit_bytes=VMEM_LIMIT)


def _silu(x):
    return x * (1.0 / (1.0 + jnp.exp(-x)))


def _dot(a, b):
    return jnp.dot(a, b, preferred_element_type=F32)


def _dot_nt(a, b):
    return lax.dot_general(a, b, (((1,), (1,)), ((), ())), preferred_element_type=F32)


def _dot_tn(a, b):
    return lax.dot_general(a, b, (((0,), (0,)), ((), ())), preferred_element_type=F32)


def _mod_norm(x, g, shift, scale):
    ms = jnp.mean(x * x, axis=-1, keepdims=True)
    return (x * lax.rsqrt(ms + EPS) * g) * (1.0 + scale) + shift


def _ada_kernel(cond_ref, w_ref, b_ref, o_ref):
    s = _silu(cond_ref[...])
    o_ref[...] = jnp.dot(s, w_ref[...], preferred_element_type=F32, precision=HIGHEST) + b_ref[...]


def _ada(cond, ada_w, ada_b):
    depth, d, n = ada_w.shape
    rows = cond.shape[0]
    tn = 1024
    return pl.pallas_call(
        _ada_kernel,
        grid=(depth, n // tn),
        in_specs=[
            pl.BlockSpec((rows, d), lambda i, j: (0, 0)),
            pl.BlockSpec((None, d, tn), lambda i, j: (i, 0, j)),
            pl.BlockSpec((None, 1, tn), lambda i, j: (i, 0, j)),
        ],
        out_specs=pl.BlockSpec((None, rows, tn), lambda i, j: (i, 0, j)),
        out_shape=jax.ShapeDtypeStruct((depth, rows, n), F32),
        compiler_params=_params(("arbitrary", "arbitrary")),
        name="ada",
    )(cond, ada_w, ada_b.reshape(depth, 1, n))


def _inproj_hy_kernel(x_ref, mod_ref, g_ref, w_ref, cw_ref, cb_ref, hy_ref, h_ref, *, period):
    seq, d = x_ref.shape
    rows = min(seq, 2 * MID_ROWS)
    for r in range(0, seq, rows):
        rs = slice(r, r + rows)
        h_ref[rs, :] = _mod_norm(x_ref[rs, :], g_ref[...], mod_ref[:, 0:d], mod_ref[:, d:2 * d]).astype(BF16)

    lane = lax.broadcasted_iota(jnp.int32, (1, seq), 1)
    first = lane % period == 0
    last = lane % period == period - 1

    def conv(a, part, c0):
        ch = slice(part * HY_W + c0, part * HY_W + c0 + a.shape[0])
        left = jnp.where(first, 0.0, pltpu.roll(a, 1, 1))
        right = jnp.where(last, 0.0, pltpu.roll(a, seq - 1, 1))
        return left * cw_ref[ch, 0:1] + a * cw_ref[ch, 1:2] + right * cw_ref[ch, 2:3] + cb_ref[ch, :]

    def hyena_part(p, combine):
        h = h_ref[...]
        sub = 8
        for c0 in range(0, HY_W, HY_CH):
            za = _dot_nt(w_ref[p, c0:c0 + HY_CH, :], h)
            zb = _dot_nt(w_ref[p, HY_W + c0:HY_W + c0 + HY_CH, :], h)
            for r in range(0, HY_CH, 2 * sub):
                pieces = [combine(za[r + i:r + i + sub, :], zb[r + i:r + i + sub, :], c0 + r + i) for i in (0, sub)]
                hy_ref[p, c0 + r:c0 + r + 2 * sub, :] = jnp.concatenate(pieces, axis=0).astype(BF16)

    hyena_part(0, lambda x0, gate, c0: conv(x0, 0, c0) * _silu(gate))
    hyena_part(1, lambda x1, v, c0: conv(x1, 1, c0) * conv(v, 2, c0))


def _inproj_hy(x2, mod4, layer, mod_row, g, w_hy, cw, cb, batch, seq, period):
    d = x2.shape[-1]
    return pl.pallas_call(
        functools.partial(_inproj_hy_kernel, period=period),
        grid=(batch,),
        in_specs=[
            pl.BlockSpec((seq, d), lambda b: (b, 0)),
            pl.BlockSpec((None, None, 1, 3 * d), lambda b: (layer, mod_row(b), 0, 0)),
            pl.BlockSpec((1, d), lambda b: (0, 0)),
            pl.BlockSpec(w_hy.shape, lambda b: (0, 0, 0), pipeline_mode=pl.Buffered(1)),
            pl.BlockSpec(cw.shape, lambda b: (0, 0), pipeline_mode=pl.Buffered(1)),
            pl.BlockSpec(cb.shape, lambda b: (0, 0), pipeline_mode=pl.Buffered(1)),
        ],
        out_specs=[
            pl.BlockSpec((2, HY_W, seq), lambda b: (0, 0, b)),
            pl.BlockSpec((seq, d), lambda b: (b, 0)),
        ],
        out_shape=[
            jax.ShapeDtypeStruct((2, HY_W, batch * seq), BF16),
            jax.ShapeDtypeStruct((batch * seq, d), BF16),
        ],
        compiler_params=_params(("arbitrary",)),
        name="inproj_hy",
    )(x2, mod4, g, w_hy, cw, cb)


def _inproj_ret_kernel(h_ref, w_ref, z_ref):
    n = w_ref.shape[1]
    step = 2 * RET_W
    for c in range(0, n, step):
        z_ref[:, c:c + step] = _dot(h_ref[...], w_ref[:, c:c + step]).astype(BF16)


def _inproj_ret(h, w_ret, tm):
    n, d = h.shape
    nr = w_ret.shape[1]
    return pl.pallas_call(
        _inproj_ret_kernel,
        grid=(n // tm,),
        in_specs=[pl.BlockSpec((tm, d), lambda i: (i, 0)), pl.BlockSpec((d, nr), lambda i: (0, 0))],
        out_specs=pl.BlockSpec((tm, nr), lambda i: (i, 0)),
        out_shape=jax.ShapeDtypeStruct((n, nr), BF16),
        compiler_params=_params(("arbitrary",)),
        name="inproj_ret",
    )(h, w_ret)


def _filter_kernel(w1t_ref, w1c_ref, w1s_ref, b1_ref, f_ref, w2_ref, b2_ref, w3_ref, bias_ref, o_ref, *, seq):
    two_l = o_ref.shape[-1]
    cb = o_ref.shape[0]
    lag = lax.broadcasted_iota(jnp.int32, (1, two_l), 1) - seq
    n = jnp.abs(lag).astype(F32)
    t = n / float(max(seq - 1, 1))
    band_i = lax.broadcasted_iota(jnp.int32, (HY_BANDS, 1), 0).astype(F32)
    bands = 1e-4 + band_i * ((HY_BANDS - 1 - 1e-4) / (HY_BANDS - 1))
    ang = (2.0 * math.pi / seq) * n * bands
    f = f_ref[...]
    pre = (w1t_ref[...] * t
           + jnp.dot(w1c_ref[...], jnp.cos(ang), preferred_element_type=F32, precision=HIGHEST)
           + jnp.dot(w1s_ref[...], -jnp.sin(ang), preferred_element_type=F32, precision=HIGHEST))
    h = jnp.sin(f * (pre + b1_ref[...]))
    h = jnp.sin(f * (jnp.dot(w2_ref[...], h, preferred_element_type=F32, precision=HIGHEST) + b2_ref[...]))
    h_b = jnp.dot(w3_ref[1], h[:, :seq], preferred_element_type=F32, precision=HIGHEST)
    h_f = jnp.dot(w3_ref[0], h[:, seq:], preferred_element_type=F32, precision=HIGHEST)
    max_decay = math.log(HY_DECAY_TARGET) / HY_FAST_PCT
    min_decay = math.log(HY_DECAY_TARGET) / HY_SLOW_PCT
    ch = (lax.broadcasted_iota(jnp.int32, (cb, 1), 0) + pl.program_id(0) * cb).astype(F32)
    deltas = jnp.abs(min_decay + ch * ((max_decay - min_decay) / (HY_W - 1)))
    window = jnp.exp(-t * deltas)
    k = jnp.concatenate([h_b, h_f], axis=1) * window
    k = jnp.where(lag == -seq, 0.0, k)
    o_ref[...] = k + jnp.where(lag == 0, bias_ref[...], 0.0)


def _filters(seq, w1, b1, freq, w2, b2, w3, bias):
    cb = 128
    w1t = w1.T
    args = (
        w1t[:, 0:1], w1t[:, 1:1 + HY_BANDS], w1t[:, 1 + HY_BANDS:],
        b1.reshape(HY_ORDER, 1), freq.reshape(HY_ORDER, 1), w2.T, b2.reshape(HY_ORDER, 1),
        w3.T.reshape(2, HY_W, HY_ORDER), bias.reshape(HY_W, 1),
    )
    full = lambda a: pl.BlockSpec(a.shape, lambda i: (0,) * a.ndim)
    in_specs = [full(a) for a in args[:7]] + [
        pl.BlockSpec((2, cb, HY_ORDER), lambda i: (0, i, 0)),
        pl.BlockSpec((cb, 1), lambda i: (i, 0)),
    ]
    return pl.pallas_call(
        functools.partial(_filter_kernel, seq=seq),
        grid=(HY_W // cb,),
        in_specs=in_specs,
        out_specs=pl.BlockSpec((cb, 2 * seq), lambda i: (i, 0)),
        out_shape=jax.ShapeDtypeStruct((HY_W, 2 * seq), F32),
        compiler_params=_params(("arbitrary",)),
        name="hyena_filters",
    )(*args)


def _hyena_kernel(*refs, seqs, batch):
    ns = len(seqs)
    ins, outs, scr = refs[:3 * ns], refs[3 * ns:4 * ns], refs[4 * ns:]
    cblk = outs[0].shape[0]
    row_i = lax.broadcasted_iota(jnp.int32, (LANES, LANES), 0)
    col_i = lax.broadcasted_iota(jnp.int32, (LANES, LANES), 1)
    upper = col_i >= row_i

    def channel(c, seq, xg_ref, u_ref, kext_ref, out_ref, t_ref, u2_ref, y2_ref):
        nb = seq // TOEP
        half = seq // LANES
        krow = kext_ref[c:c + 1, :]

        def rotated(q):
            a = jnp.broadcast_to(krow[:, q * LANES:(q + 1) * LANES], (LANES, LANES))
            return pltpu.roll(a, 0, 1, stride=1, stride_axis=0)

        e_lo = -(2 * nb - 1)
        r_prev = rotated(half + e_lo - 1)
        for e in range(e_lo, 2 * nb):
            r_cur = rotated(half + e)
            g = jnp.where(upper, r_cur, r_prev).astype(BF16)
            r_prev = r_cur
            if e % 2 == 0:
                d = e // 2
                t_ref[d + nb - 1, 0:LANES, 0:LANES] = g
                t_ref[d + nb - 1, LANES:TOEP, LANES:TOEP] = g
            else:
                d = (e - 1) // 2
                if abs(d) <= nb - 1:
                    t_ref[d + nb - 1, 0:LANES, LANES:TOEP] = g
                d = (e + 1) // 2
                if abs(d) <= nb - 1:
                    t_ref[d + nb - 1, LANES:TOEP, 0:LANES] = g
        for j in range(nb):
            u2_ref[j * batch:(j + 1) * batch, :] = u_ref[c, :, j * TOEP:(j + 1) * TOEP]
        y2_ref[...] = _dot(u2_ref[...], t_ref[nb - 1])
        for d in list(range(1, nb)) + list(range(-(nb - 1), 0)):
            j_lo, j_hi = max(0, -d), nb - max(0, d)
            res = _dot(u2_ref[j_lo * batch:j_hi * batch, :], t_ref[d + nb - 1])
            y2_ref[(j_lo + d) * batch:(j_hi + d) * batch, :] += res
        for j in range(nb):
            sl = slice(j * TOEP, (j + 1) * TOEP)
            out_ref[c, :, sl] = (xg_ref[c, :, sl].astype(F32) * y2_ref[j * batch:(j + 1) * batch, :]).astype(BF16)

    for c in range(cblk):
        for s, seq in enumerate(seqs):
            t_all, u2_all, y2_all = scr[3 * s:3 * s + 3]
            channel(c, seq, *ins[3 * s:3 * s + 3], outs[s], t_all.at[c], u2_all.at[c], y2_all.at[c])


def _hyena(hys, kexts, batch, seqs):
    cblk = 8
    args, in_specs, out_specs, out_shape, scratch = [], [], [], [], []
    for hy, kext, seq in zip(hys, kexts, seqs):
        nb = seq // TOEP
        hy4 = hy.reshape(2, HY_W, batch, seq)
        part = lambda p, seq=seq: pl.BlockSpec((None, cblk, batch, seq), lambda i: (p, i, 0, 0))
        args += [hy4, hy4, kext]
        in_specs += [part(0), part(1), pl.BlockSpec((cblk, 2 * seq), lambda i: (i, 0))]
        out_specs.append(pl.BlockSpec((cblk, batch, seq), lambda i: (i, 0, 0)))
        out_shape.append(jax.ShapeDtypeStruct((HY_W, batch, seq), BF16))
        scratch += [pltpu.VMEM((cblk, 2 * nb - 1, TOEP, TOEP), BF16),
                    pltpu.VMEM((cblk, nb * batch, TOEP), BF16),
                    pltpu.VMEM((cblk, nb * batch, TOEP), F32)]
    return pl.pallas_call(
        functools.partial(_hyena_kernel, seqs=tuple(seqs), batch=batch),
        grid=(HY_W // cblk,),
        in_specs=in_specs,
        out_specs=out_specs,
        out_shape=out_shape,
        scratch_shapes=scratch,
        compiler_params=_params(("arbitrary",)),
        name="hyena",
    )(*args)


def _retention_tables_kernel(af_ref, ab_ref, tab_ref):
    k_scale = RET_HD ** -0.5
    ri = lax.broadcasted_iota(jnp.int32, (CHUNK, CHUNK), 0).astype(F32)
    ci = lax.broadcasted_iota(jnp.int32, (CHUNK, CHUNK), 1).astype(F32)
    diff = ri - ci
    for h in range(RET_H):
        log_f = -jnp.exp(jnp.full((CHUNK, CHUNK), af_ref[h], F32))
        log_b = -jnp.exp(jnp.full((CHUNK, CHUNK), ab_ref[h], F32))
        dec_f = jnp.where(diff >= 0, jnp.exp(log_f * jnp.maximum(diff, 0.0)), 0.0)
        dec_b = jnp.where(diff <= 0, jnp.exp(log_b * jnp.maximum(-diff, 0.0)), 0.0)
        tab_ref[h, T_DEC] = (dec_f + dec_b) * k_scale
        tab_ref[h, T_QDF] = jnp.exp(log_f * (ri + 1.0))
        tab_ref[h, T_QDB] = jnp.exp(log_b * (CHUNK - ri))
        tab_ref[h, T_KDF] = jnp.exp(log_f * (CHUNK - 1.0 - ri)) * k_scale
        tab_ref[h, T_KDB] = jnp.exp(log_b * ri) * k_scale
        tab_ref[h, T_CDF] = jnp.exp(log_f * CHUNK)
        tab_ref[h, T_CDB] = jnp.exp(log_b * CHUNK)


def _retention_tables(a_f, a_b):
    smem = pl.BlockSpec(memory_space=pltpu.SMEM)
    return pl.pallas_call(
        _retention_tables_kernel,
        in_specs=[smem, smem],
        out_shape=jax.ShapeDtypeStruct((RET_H, N_TABLES, CHUNK, CHUNK), F32),
        name="retention_tables",
    )(a_f, a_b)


def _retention_kernel(tab_ref, zc_ref, zx_ref, ng_ref, oc_ref, ox_ref, sf_ref, sb_ref, sbc_ref, sbx_ref):
    hd = RET_HD
    sf_ref[...] = jnp.zeros_like(sf_ref)
    sb_ref[...] = jnp.zeros_like(sb_ref)

    def head_cols(part, h):
        return slice((part * RET_H + h) * hd, (part * RET_H + h + 1) * hd)

    def state_update(s, k, v, kd, cd):
        kdt = jnp.transpose(k.astype(F32) * kd).astype(BF16)
        return s * cd + _dot(kdt, v)

    def state_pass(z_ref, store_ref):
        nchunk = z_ref.shape[0] // CHUNK

        for i in range(nchunk):
            for t, s_ref, lo, t_kd, t_cd in ((i, sf_ref, 0, T_KDF, T_CDF),
                                             (nchunk - 1 - i, sb_ref, hd, T_KDB, T_CDB)):
                rows = slice(t * CHUNK, (t + 1) * CHUNK)
                for h in range(RET_H):
                    k = z_ref[rows, head_cols(1, h)]
                    v = z_ref[rows, head_cols(2, h)]
                    s = s_ref[h]
                    store_ref[t, h, :, lo:lo + hd] = s.astype(BF16)
                    s_ref[h] = state_update(s, k, v, tab_ref[h, t_kd], tab_ref[h, t_cd])

    def out_pass(z_ref, store_ref, out_ref):
        nchunk = z_ref.shape[0] // CHUNK

        def first_dots(t, h):
            rows = slice(t * CHUNK, (t + 1) * CHUNK)
            q = z_ref[rows, head_cols(0, h)]
            k = z_ref[rows, head_cols(1, h)]
            return _dot_nt(q, k), _dot(q, store_ref[t, h])

        def finish(t, h, scores, cross):
            rows = slice(t * CHUNK, (t + 1) * CHUNK)
            v = z_ref[rows, head_cols(2, h)]
            o = (_dot((scores * tab_ref[h, T_DEC]).astype(BF16), v) + cross[:, :hd] * tab_ref[h, T_QDF]
                 + cross[:, hd:] * tab_ref[h, T_QDB])
            mu = jnp.mean(o, axis=-1, keepdims=True)
            oc = o - mu
            var = jnp.mean(oc * oc, axis=-1, keepdims=True)
            y = oc * lax.rsqrt(var + EPS) * ng_ref[:, h * hd:(h + 1) * hd]
            gate = _silu(z_ref[rows, head_cols(3, h)].astype(F32))
            out_ref[rows, h * hd:(h + 1) * hd] = (y * gate).astype(BF16)

        items = [(t, h) for t in range(nchunk) for h in range(RET_H)]
        dots = first_dots(*items[0])
        for i, item in enumerate(items):
            nxt = first_dots(*items[i + 1]) if i + 1 < len(items) else None
            finish(*item, *dots)
            dots = nxt

    state_pass(zc_ref, sbc_ref)
    state_pass(zx_ref, sbx_ref)
    out_pass(zc_ref, sbc_ref, oc_ref)
    out_pass(zx_ref, sbx_ref, ox_ref)


def _retention(zret_c, zret_x, a_f, a_b, norm_g, batch, ctx_len, seq):
    hd = RET_HD
    tables = _retention_tables(a_f, a_b)
    return pl.pallas_call(
        _retention_kernel,
        grid=(batch,),
        in_specs=[pl.BlockSpec(tables.shape, lambda b: (0, 0, 0, 0)),
                  pl.BlockSpec((ctx_len, 4 * RET_W), lambda b: (b, 0)),
                  pl.BlockSpec((seq, 4 * RET_W), lambda b: (b, 0)),
                  pl.BlockSpec((1, RET_W), lambda b: (0, 0))],
        out_specs=[pl.BlockSpec((ctx_len, RET_W), lambda b: (b, 0)),
                   pl.BlockSpec((seq, RET_W), lambda b: (b, 0))],
        out_shape=[jax.ShapeDtypeStruct((batch * ctx_len, RET_W), BF16),
                   jax.ShapeDtypeStruct((batch * seq, RET_W), BF16)],
        scratch_shapes=[
            pltpu.VMEM((RET_H, hd, hd), F32),
            pltpu.VMEM((RET_H, hd, hd), F32),
            pltpu.VMEM((ctx_len // CHUNK, RET_H, hd, 2 * hd), BF16),
            pltpu.VMEM((seq // CHUNK, RET_H, hd, 2 * hd), BF16),
        ],
        compiler_params=_params(("arbitrary",)),
        name="retention",
    )(tables, zret_c, zret_x, norm_g.reshape(1, RET_W))


def _head_norm_rope(z, gain, cos, sin):
    ms = jnp.mean(z * z, axis=-1, keepdims=True)
    y = z * lax.rsqrt(ms + EPS) * gain
    if cos is None:
        return y
    return y * cos + pltpu.roll(y, HEAD_DIM // 2, 1) * sin


def _store_v_aug(v_ref, v):
    ones = jnp.ones((v.shape[0], HEAD_DIM), BF16)
    for hd in range(N_KV):
        v_ref[:, 2 * hd * HEAD_DIM:(2 * hd + 1) * HEAD_DIM] = v[:, hd * HEAD_DIM:(hd + 1) * HEAD_DIM].astype(BF16)
        v_ref[:, (2 * hd + 1) * HEAD_DIM:(2 * hd + 2) * HEAD_DIM] = ones


def _mid_kernel(x_ref, yhyt_ref, yret_ref, mod0_ref, mod1_ref, wo_hy_ref, wo_ret_ref, g1_ref, win_ref,
                qn_ref, kn_ref, *rest, latent):
    d = x_ref.shape[-1]
    if latent:
        cos_ref, sin_ref, xo_ref, q_ref, k_ref, v_ref, g_ref = rest
    else:
        k_ref, v_ref = rest
    tm = x_ref.shape[0]
    sub = min(tm, MID_ROWS)
    pair = 2 * HEAD_DIM

    def front(rs):
        y = _dot_tn(yhyt_ref[:, rs], wo_hy_ref[...]) + _dot(yret_ref[rs, :], wo_ret_ref[...])
        x = x_ref[rs, :] + mod0_ref[:, 2 * d:3 * d] * y
        if latent:
            xo_ref[rs, :] = x
        return _mod_norm(x, g1_ref[...], mod1_ref[:, 0:d], mod1_ref[:, d:2 * d]).astype(BF16)

    def back(rs, h):
        cos, sin = (cos_ref[rs, :], sin_ref[rs, :]) if latent else (None, None)
        pieces = []

        def q_pair(off):
            z2 = _dot(h, win_ref[:, off:off + pair])
            for i in range(2):
                q = _head_norm_rope(z2[:, i * HEAD_DIM:(i + 1) * HEAD_DIM], qn_ref[...], cos, sin) * Q_PRESCALE
                q_ref[rs, off + i * HEAD_DIM:off + (i + 1) * HEAD_DIM] = q.astype(BF16)

        def k_pair(off):
            z2 = _dot(h, win_ref[:, off:off + pair])
            for i in range(N_KV):
                k = _head_norm_rope(z2[:, i * HEAD_DIM:(i + 1) * HEAD_DIM], kn_ref[...], cos, sin)
                k_ref[rs, i * HEAD_DIM:(i + 1) * HEAD_DIM] = k.astype(BF16)

        def v_part(off):
            _store_v_aug(v_ref.at[rs, :], _dot(h, win_ref[:, off:off + KV_W]))

        def g_part(off, c):
            g_ref[rs, c:c + pair] = _dot(h, win_ref[:, off + c:off + c + pair]).astype(BF16)

        off = 0
        if latent:
            for hp in range(N_HEADS // 2):
                pieces.append(functools.partial(q_pair, off))
                off += pair
        pieces.append(functools.partial(k_pair, off))
        off += pair
        pieces.append(functools.partial(v_part, off))
        off += KV_W
        if latent:
            for c in range(0, ATT_W, pair):
                pieces.append(functools.partial(g_part, off, c))
        return pieces

    groups = [slice(r, r + sub) for r in range(0, tm, sub)]
    h = front(groups[0])
    for gi, rs in enumerate(groups):
        pieces = back(rs, h)
        cut = len(pieces) // 4
        for piece in pieces[:cut]:
            piece()
        if gi + 1 < len(groups):
            h = front(groups[gi + 1])
        for piece in pieces[cut:]:
            piece()


def _mid(x2, yhyt, yret, mod4, mod_row, wo_hy, wo_ret, g1, win, qn, kn, cosf, sinf, batch, seq, tm, latent):
    d = x2.shape[-1]
    nt = seq // tm
    tok = lambda w: pl.BlockSpec((tm, w), lambda b, t: (b * nt + t, 0))
    const = lambda a: pl.BlockSpec(a.shape, lambda b, t: (0,) * a.ndim)
    modspec = lambda layer: pl.BlockSpec((None, None, 1, 3 * d), lambda b, t: (layer, mod_row(b), 0, 0))
    in_specs = [tok(d), pl.BlockSpec((HY_W, tm), lambda b, t: (0, b * nt + t)), tok(RET_W),
                modspec(0), modspec(1), const(wo_hy), const(wo_ret), const(g1), const(win), const(qn), const(kn)]
    args = [x2, yhyt, yret, mod4, mod4, wo_hy, wo_ret, g1, win, qn, kn]
    n = batch * seq
    if latent:
        in_specs += [pl.BlockSpec((tm, HEAD_DIM), lambda b, t: (t, 0))] * 2
        args += [cosf, sinf]
        out_specs = [tok(d), tok(ATT_W), tok(KV_W), tok(2 * KV_W), tok(ATT_W)]
        out_shape = [jax.ShapeDtypeStruct((n, d), F32), jax.ShapeDtypeStruct((n, ATT_W), BF16),
                     jax.ShapeDtypeStruct((n, KV_W), BF16), jax.ShapeDtypeStruct((n, 2 * KV_W), BF16),
                     jax.ShapeDtypeStruct((n, ATT_W), BF16)]
    else:
        out_specs = [tok(KV_W), tok(2 * KV_W)]
        out_shape = [jax.ShapeDtypeStruct((n, KV_W), BF16), jax.ShapeDtypeStruct((n, 2 * KV_W), BF16)]
    return pl.pallas_call(
        functools.partial(_mid_kernel, latent=latent),
        grid=(batch, nt),
        in_specs=in_specs,
        out_specs=out_specs,
        out_shape=out_shape,
        compiler_params=_params(("arbitrary", "arbitrary")),
        name="mid_latent" if latent else "mid_ctx",
    )(*args)


def _attn_kernel(q_ref, kc_ref, vc_ref, kx_ref, vx_ref, g_ref, x_ref, mod_ref, w_ref, fg_ref, out_ref):
    d = x_ref.shape[-1]
    heads = []

    def scores(hd):
        ksl = slice((hd // GROUP) * HEAD_DIM, (hd // GROUP + 1) * HEAD_DIM)
        q = q_ref[:, hd * HEAD_DIM:(hd + 1) * HEAD_DIM]
        return _dot_nt(q, kc_ref[:, ksl]), _dot_nt(q, kx_ref[:, ksl])

    s_next = scores(0)
    for hd in range(N_HEADS):
        kvh = hd // GROUP
        vsl = slice(2 * kvh * HEAD_DIM, 2 * (kvh + 1) * HEAD_DIM)
        sl = slice(hd * HEAD_DIM, (hd + 1) * HEAD_DIM)
        s_c, s_x = s_next
        if hd + 1 < N_HEADS:
            s_next = scores(hd + 1)
        m = jnp.maximum(jnp.max(s_c, axis=-1, keepdims=True), jnp.max(s_x, axis=-1, keepdims=True))
        p_c = jnp.exp2(s_c - m).astype(BF16)
        p_x = jnp.exp2(s_x - m).astype(BF16)
        o = _dot(p_c, vc_ref[:, vsl]) + _dot(p_x, vx_ref[:, vsl])
        gate = _silu(g_ref[:, sl].astype(F32))
        heads.append((gate * (o[:, :HEAD_DIM] / o[:, HEAD_DIM:])).astype(BF16))
    y = _dot(jnp.concatenate(heads, axis=1), w_ref[...])
    x = x_ref[...] + mod_ref[:, 2 * d:3 * d] * y
    ms = jnp.mean(x * x, axis=-1, keepdims=True)
    out_ref[...] = x * lax.rsqrt(ms + EPS) * fg_ref[...]


def _attention(q, kc, vc, kx, vx, g, x2, mod4, w, fg, batch, ctx_len, seq, tq):
    d = x2.shape[-1]
    nt = seq // tq
    kv = lambda rows, w_: pl.BlockSpec((rows, w_), lambda b, t: (b, 0))
    tok = lambda w_: pl.BlockSpec((tq, w_), lambda b, t: (b * nt + t, 0))
    const = lambda a: pl.BlockSpec(a.shape, lambda b, t: (0,) * a.ndim)
    return pl.pallas_call(
        _attn_kernel,
        grid=(batch, nt),
        in_specs=[tok(ATT_W), kv(ctx_len, KV_W), kv(ctx_len, 2 * KV_W), kv(seq, KV_W), kv(seq, 2 * KV_W),
                  tok(ATT_W), tok(d),
                  pl.BlockSpec((None, None, 1, 3 * d), lambda b, t: (1, b, 0, 0)), const(w), const(fg)],
        out_specs=tok(d),
        out_shape=jax.ShapeDtypeStruct((batch * seq, d), F32),
        compiler_params=_params(("arbitrary", "arbitrary")),
        name="attention",
    )(q, kc, vc, kx, vx, g, x2, mod4, w, fg)


def _rope_tables(seq):
    rows = seq // GRID_W
    r = jnp.repeat(jnp.arange(rows, dtype=F32), GRID_W)
    col = jnp.tile(jnp.arange(GRID_W, dtype=F32), rows)
    half = HEAD_DIM // 2
    inv = ROPE_THETA ** (-jnp.arange(0, half, 2, dtype=F32) / half)
    ang = jnp.concatenate([r[:, None] * inv, col[:, None] * inv], axis=-1)
    cos, sin = jnp.cos(ang), jnp.sin(ang)
    return jnp.concatenate([cos, cos], axis=-1), jnp.concatenate([-sin, sin], axis=-1)


def kernel(x, c, ctx, c_ctx, norm_g, ada_w, ada_b, er_in_w, er_out_w, hy_conv_w, hy_conv_b, hy_f_w1, hy_f_b1, hy_f_freq, hy_f_w2, hy_f_b2, hy_f_w3, hy_bias, ret_decay_f, ret_decay_b, ret_norm_g, at_in_w, at_out_w, at_q_norm, at_k_norm, final_norm_g):
    batch, seq, d = x.shape
    ctx_len = ctx.shape[1]
    assert seq % TOEP == 0 and ctx_len % TOEP == 0 and seq % GRID_W == 0 and d == 2 * HY_W
    tm = min(512, seq)

    rows = -(-(batch + 1) // 8) * 8
    cond = jnp.concatenate([c, c_ctx[None, :], jnp.zeros((rows - batch - 1, d), F32)], axis=0)
    mod4 = _ada(cond, ada_w, ada_b).reshape(ada_w.shape[0], rows, 1, 3 * d)
    lat_row = lambda b: b
    ctx_row = lambda b: batch

    x2 = x.reshape(batch * seq, d)
    c2 = ctx.reshape(batch * ctx_len, d)

    in_w = er_in_w[0].astype(BF16)
    col = lambda p: in_w[:, p * HY_W:(p + 1) * HY_W].T
    w_hy = jnp.stack([jnp.concatenate([col(0), col(3)], axis=0), jnp.concatenate([col(1), col(2)], axis=0)])
    w_ret = in_w[:, 4 * HY_W:]
    cw = hy_conv_w[0].T
    cb = hy_conv_b[0].reshape(3 * HY_W, 1)
    g0 = norm_g[0].reshape(1, d)
    hy_x, h_x = _inproj_hy(x2, mod4, 0, lat_row, g0, w_hy, cw, cb, batch, seq, seq)
    ctx_rows = math.gcd(batch * ctx_len, seq)
    hy_c, h_c = _inproj_hy(c2, mod4, 0, ctx_row, g0, w_hy, cw, cb, batch * ctx_len // ctx_rows, ctx_rows, ctx_len)
    zret_x = _inproj_ret(h_x, w_ret, min(2 * tm, h_x.shape[0]))
    zret_c = _inproj_ret(h_c, w_ret, min(2 * tm, h_c.shape[0]))

    filt = (hy_f_w1[0], hy_f_b1[0], hy_f_freq[0], hy_f_w2[0], hy_f_b2[0], hy_f_w3[0], hy_bias[0])
    yhyt_x, yhyt_c = _hyena((hy_x, hy_c), (_filters(seq, *filt), _filters(ctx_len, *filt)), batch, (seq, ctx_len))
    yret_c, yret_x = _retention(zret_c, zret_x, ret_decay_f[0], ret_decay_b[0], ret_norm_g[0], batch, ctx_len, seq)

    wo_hy = er_out_w[0][:HY_W].astype(BF16)
    wo_ret = er_out_w[0][HY_W:].astype(BF16)
    perm = jnp.concatenate([jnp.arange(0, HEAD_DIM, 2), jnp.arange(1, HEAD_DIM, 2)])
    qk_cols = (jnp.arange(N_HEADS + N_KV)[:, None] * HEAD_DIM + perm[None, :]).reshape(-1)
    w1 = at_in_w[0]
    win_x = jnp.concatenate([w1[:, qk_cols], w1[:, ATT_W + KV_W:]], axis=1).astype(BF16)
    win_c = win_x[:, ATT_W:ATT_W + 2 * KV_W]
    qn = at_q_norm[0][perm].reshape(1, HEAD_DIM)
    kn = at_k_norm[0][perm].reshape(1, HEAD_DIM)
    g1 = norm_g[1].reshape(1, d)
    cosf, sinf = _rope_tables(seq)
    x1, q, kx, vx, gate = _mid(x2, yhyt_x.reshape(HY_W, batch * seq), yret_x, mod4, lat_row, wo_hy, wo_ret, g1,
                               win_x, qn, kn, cosf, sinf, batch, seq, tm, True)
    kc, vc = _mid(c2, yhyt_c.reshape(HY_W, batch * ctx_len), yret_c, mod4, ctx_row, wo_hy, wo_ret, g1,
                  win_c, qn, kn, None, None, batch, ctx_len, ctx_len, False)

    out = _attention(q, kc, vc, kx, vx, gate, x1, mod4, at_out_w[0].astype(BF16), final_norm_g.reshape(1, d),
                     batch, ctx_len, seq, min(1024, seq))
    return out.reshape(batch, seq, d)
```
